```python
import math
import jax, jax.numpy as jnp
from jax import lax
import numpy as np

D_MODEL = 1024
BATCH = 4
SEQ = 8192
DEPTH = 2

N_BRANCH = 4
BRANCH_W = D_MODEL // 2
N_HEADS = 4
CHUNK = 64
EPS = 1e-6

GLA_DK = BRANCH_W // (2 * N_HEADS)
GLA_DV = BRANCH_W // N_HEADS
GLA_RANK = 16
GLA_TAU = 16.0

GDN_DK = BRANCH_W // N_HEADS
GDN_DV = BRANCH_W // N_HEADS
CONV_K = 4

RET_DK = BRANCH_W // (2 * N_HEADS)
RET_DV = BRANCH_W // N_HEADS
ROPE_BASE = 10000.0

S5_GROUP = 16
S5_GROUPS = BRANCH_W // S5_GROUP
S5_STATE = 64

IN_SIZES = (
    N_HEADS * GLA_DK, N_HEADS * GLA_DK, BRANCH_W, GLA_RANK, BRANCH_W,
    2 * N_HEADS * GDN_DK + N_HEADS * GDN_DV, N_HEADS, N_HEADS, BRANCH_W,
    N_HEADS * RET_DK, N_HEADS * RET_DK, BRANCH_W, BRANCH_W,
    BRANCH_W, BRANCH_W,
    N_BRANCH * D_MODEL,
)
D_IN = sum(IN_SIZES)

kernel_name = 'gated_parallel_hybrid_mixer'


def rms_norm(x, g):
    xf = x.astype(jnp.float32)
    return xf * lax.rsqrt(jnp.mean(jnp.square(xf), axis=-1, keepdims=True) + EPS) * g


def group_norm(x, g):
    xf = x.astype(jnp.float32)
    xc = xf - jnp.mean(xf, axis=-1, keepdims=True)
    return xc * lax.rsqrt(jnp.mean(jnp.square(xc), axis=-1, keepdims=True) + EPS) * g


def l2_normalize(x):
    xf = x.astype(jnp.float32)
    return xf * lax.rsqrt(jnp.sum(jnp.square(xf), axis=-1, keepdims=True) + EPS)


def split_heads(x, d):
    b, t, _ = x.shape
    return x.reshape(b, t, -1, d).transpose(0, 2, 1, 3)


def merge_heads(x):
    b, h, t, d = x.shape
    return x.transpose(0, 2, 1, 3).reshape(b, t, h * d)


def to_chunks(x):
    b, h, t = x.shape[:3]
    return x.reshape((b, h, t // CHUNK, CHUNK) + x.shape[3:])


def split_cols(p, sizes):
    return jnp.split(p, [int(s) for s in np.cumsum(sizes)[:-1]], axis=-1)


def chunk_gated_linear_attention(q, k, v, log_a):
    b, h, t, dk = q.shape
    dv = v.shape[-1]
    qc, kc, vc = to_chunks(q), to_chunks(k), to_chunks(v)
    cum = jnp.cumsum(to_chunks(log_a).astype(jnp.float32), axis=-2)
    last = cum[..., -1:, :]
    q_e = qc * jnp.exp(cum)
    k_e = kc * jnp.exp(-cum)
    k_s = kc * jnp.exp(last - cum)
    causal = jnp.tril(jnp.ones((CHUNK, CHUNK), dtype=bool))
    scores = jnp.where(causal, jnp.einsum('bhncd,bhnsd->bhncs', q_e, k_e), 0.0)
    o_intra = jnp.einsum('bhncs,bhnsv->bhncv', scores, vc)
    kv = jnp.einsum('bhncd,bhncv->bhndv', k_s, vc)
    decay = jnp.exp(last[..., 0, :])

    def step(s, inp):
        dec, kv_n = inp
        return dec[..., None] * s + kv_n, s

    _, s_prev = lax.scan(step, jnp.zeros((b, h, dk, dv), jnp.float32),
                         (jnp.moveaxis(decay, 2, 0), jnp.moveaxis(kv, 2, 0)))
    o_inter = jnp.einsum('bhncd,bhndv->bhncv', q_e, jnp.moveaxis(s_prev, 0, 2))
    return (o_intra + o_inter).reshape(b, h, t, dv)


def chunk_gated_delta_rule(q, k, v, g, beta):
    b, h, t, dk = q.shape
    dv = v.shape[-1]
    qc, kc, vc = to_chunks(q), to_chunks(k), to_chunks(v).astype(jnp.float32)
    gc = jnp.cumsum(to_chunks(g), axis=-1)
    bc = to_chunks(beta)
    incl = jnp.tril(jnp.ones((CHUNK, CHUNK), dtype=bool))
    strict = jnp.tril(jnp.ones((CHUNK, CHUNK), dtype=bool), k=-1)
    decay = jnp.exp(jnp.where(incl, gc[..., :, None] - gc[..., None, :], -jnp.inf))
    k_beta = kc * bc[..., None]
    low = jnp.where(strict, jnp.einsum('bhncd,bhnsd->bhncs', k_beta, kc) * decay, 0.0)
    tri = low + jnp.eye(CHUNK, dtype=low.dtype)
    u = lax.linalg.triangular_solve(tri, vc * bc[..., None], left_side=True, lower=True, unit_diagonal=True)
    w = lax.linalg.triangular_solve(tri, k_beta * jnp.exp(gc)[..., None], left_side=True, lower=True, unit_diagonal=True)
    attn = jnp.einsum('bhncd,bhnsd->bhncs', qc, kc) * decay
    q_e = qc * jnp.exp(gc)[..., None]
    k_s = kc * jnp.exp(gc[..., -1:] - gc)[..., None]
    dec = jnp.exp(gc[..., -1])

    def step(s, inp):
        w_n, u_n, qe_n, att_n, ks_n, dec_n = inp
        v_new = u_n - jnp.einsum('bhck,bhkv->bhcv', w_n, s)
        o = jnp.einsum('bhck,bhkv->bhcv', qe_n, s) + jnp.einsum('bhcs,bhsv->bhcv', att_n, v_new)
        s = s * dec_n[..., None, None] + jnp.einsum('bhck,bhcv->bhkv', ks_n, v_new)
        return s, o

    xs = tuple(jnp.moveaxis(a, 2, 0) for a in (w, u, q_e, attn, k_s, dec))
    _, o = lax.scan(step, jnp.zeros((b, h, dk, dv), jnp.float32), xs)
    return jnp.moveaxis(o, 0, 2).reshape(b, h, t, dv)


def rotary(x):
    t, d = x.shape[2], x.shape[3]
    half = d // 2
    inv_freq = ROPE_BASE ** (-jnp.arange(half, dtype=jnp.float32) / half)
    ang = jnp.arange(t, dtype=jnp.float32)[:, None] * inv_freq
    cos, sin = jnp.cos(ang), jnp.sin(ang)
    x1, x2 = x[..., :half], x[..., half:]
    return jnp.concatenate([x1 * cos - x2 * sin, x1 * sin + x2 * cos], axis=-1)


def causal_depthwise_conv(x, w):
    c = x.shape[-1]
    return lax.conv_general_dilated(x, w[:, None, :].astype(x.dtype), window_strides=(1,),
                                    padding=[(CONV_K - 1, 0)], dimension_numbers=('NWC', 'WIO', 'NWC'),
                                    feature_group_count=c)


def gla_branch(q, k, v, lr, z, w_a2, b_a, g_norm):
    q = split_heads(q, GLA_DK) * GLA_DK ** -0.5
    k = split_heads(k, GLA_DK)
    v = split_heads(v, GLA_DV)
    log_a = jax.nn.log_sigmoid((lr @ w_a2 + b_a).astype(jnp.float32)) / GLA_TAU
    o = chunk_gated_linear_attention(q, k, v, split_heads(log_a, GLA_DK))
    return merge_heads(rms_norm(o, g_norm)) * jax.nn.silu(z)


def gdn_branch(qkv, a, bt, z, conv_w, a_log, dt_bias, g_norm):
    qkv = jax.nn.silu(causal_depthwise_conv(qkv, conv_w))
    q, k, v = jnp.split(qkv, [N_HEADS * GDN_DK, 2 * N_HEADS * GDN_DK], axis=-1)
    q = l2_normalize(split_heads(q, GDN_DK)) * GDN_DK ** -0.5
    k = l2_normalize(split_heads(k, GDN_DK))
    v = split_heads(v, GDN_DV)
    g = -jnp.exp(a_log) * jax.nn.softplus(a.astype(jnp.float32) + dt_bias)
    beta = jax.nn.sigmoid(bt.astype(jnp.float32))
    o = chunk_gated_delta_rule(q, k, v, g.transpose(0, 2, 1), beta.transpose(0, 2, 1))
    return merge_heads(rms_norm(o, g_norm)) * jax.nn.silu(z)


def retention_branch(q, k, v, z, g_norm):
    q = rotary(split_heads(q, RET_DK))
    k = rotary(split_heads(k, RET_DK)) * RET_DK ** -0.5
    v = split_heads(v, RET_DV)
    log_gamma = jnp.log1p(-jnp.exp2(-5.0 - jnp.arange(N_HEADS, dtype=jnp.float32)))
    log_a = jnp.broadcast_to(log_gamma[None, :, None, None], q.shape)
    o = chunk_gated_linear_attention(q, k, v, log_a)
    return merge_heads(group_norm(o, g_norm)) * jax.nn.silu(z)


def _complex_linear_combine(e1, e2):
    a1r, a1i, b1r, b1i = e1
    a2r, a2i, b2r, b2i = e2
    return (a1r * a2r - a1i * a2i, a1r * a2i + a1i * a2r,
            a2r * b1r - a2i * b1i + b2r, a2r * b1i + a2i * b1r + b2i)


def s5_branch(u, z, lam_re, lam_im, log_dt, b_re, b_im, c_re, c_im, d, w_glu, b_glu):
    bsz, t, _ = u.shape
    ug = u.reshape(bsz, t, S5_GROUPS, S5_GROUP)
    dt = jnp.exp(log_dt)[:, None]
    mag = jnp.exp(lam_re * dt)
    abar_re, abar_im = mag * jnp.cos(lam_im * dt), mag * jnp.sin(lam_im * dt)
    nr, ni = abar_re - 1.0, abar_im
    den = jnp.square(lam_re) + jnp.square(lam_im)
    coef_re = (nr * lam_re + ni * lam_im) / den
    coef_im = (ni * lam_re - nr * lam_im) / den
    bbar_re = coef_re[..., None] * b_re - coef_im[..., None] * b_im
    bbar_im = coef_re[..., None] * b_im + coef_im[..., None] * b_re
    bu_re = jnp.einsum('btgi,gpi->tbgp', ug, bbar_re)
    bu_im = jnp.einsum('btgi,gpi->tbgp', ug, bbar_im)
    a_re = jnp.broadcast_to(abar_re[None, None], (t, 1, S5_GROUPS, S5_STATE))
    a_im = jnp.broadcast_to(abar_im[None, None], (t, 1, S5_GROUPS, S5_STATE))
    _, _, x_re, x_im = lax.associative_scan(_complex_linear_combine, (a_re, a_im, bu_re, bu_im), axis=0)
    y = jnp.einsum('tbgp,gip->btgi', x_re, c_re) - jnp.einsum('tbgp,gip->btgi', x_im, c_im)
    y = y.reshape(bsz, t, BRANCH_W) + d * u
    y = jax.nn.gelu(y)
    y = y * jax.nn.sigmoid(y @ w_glu + b_glu)
    return y * jax.nn.silu(z)


def setup_inputs(seed: int = 0) -> dict:
    key = jax.random.key(seed)
    ks = jax.random.split(key, 24)
    f32 = jnp.float32
    nrm = lambda k, s: jax.random.normal(k, s, f32)
    dt_g = jnp.exp(jax.random.uniform(ks[9], (DEPTH, N_HEADS), f32, math.log(1e-3), math.log(1e-1)))
    return {
        'x': nrm(ks[0], (BATCH, SEQ, D_MODEL)),
        'norm_g': 1.0 + 0.02 * nrm(ks[1], (DEPTH, D_MODEL)),
        'w_in': nrm(ks[2], (DEPTH, D_MODEL, D_IN)) * D_MODEL ** -0.5,
        'w_branch': nrm(ks[3], (DEPTH, N_BRANCH, BRANCH_W, D_MODEL)) * BRANCH_W ** -0.5,
        'w_out': nrm(ks[4], (DEPTH, D_MODEL, D_MODEL)) * (2.0 * D_MODEL) ** -0.5,
        'gla_w_a2': nrm(ks[5], (DEPTH, GLA_RANK, N_HEADS * GLA_DK)) * GLA_RANK ** -0.5,
        'gla_b_a': 0.02 * nrm(ks[6], (DEPTH, N_HEADS * GLA_DK)),
        'gla_norm': 1.0 + 0.02 * nrm(ks[7], (DEPTH, GLA_DV)),
        'gdn_conv': nrm(ks[8], (DEPTH, CONV_K, 2 * N_HEADS * GDN_DK + N_HEADS * GDN_DV)) * CONV_K ** -0.5,
        'gdn_a_log': jnp.log(jax.random.uniform(ks[10], (DEPTH, N_HEADS), f32, 1.0, 16.0)),
        'gdn_dt_bias': dt_g + jnp.log(-jnp.expm1(-dt_g)),
        'gdn_norm': 1.0 + 0.02 * nrm(ks[11], (DEPTH, GDN_DV)),
        'ret_norm': 1.0 + 0.02 * nrm(ks[12], (DEPTH, RET_DV)),
        's5_lambda_re': -0.5 + 0.01 * nrm(ks[13], (DEPTH, S5_GROUPS, S5_STATE)),
        's5_lambda_im': math.pi * jnp.arange(S5_STATE, dtype=f32) + 0.01 * nrm(ks[14], (DEPTH, S5_GROUPS, S5_STATE)),
        's5_log_dt': jax.random.uniform(ks[15], (DEPTH, S5_GROUPS), f32, math.log(1e-3), math.log(1e-1)),
        's5_b_re': nrm(ks[16], (DEPTH, S5_GROUPS, S5_STATE, S5_GROUP)) * (2.0 * S5_GROUP) ** -0.5,
        's5_b_im': nrm(ks[17], (DEPTH, S5_GROUPS, S5_STATE, S5_GROUP)) * (2.0 * S5_GROUP) ** -0.5,
        's5_c_re': nrm(ks[18], (DEPTH, S5_GROUPS, S5_GROUP, S5_STATE)) * S5_STATE ** -0.5,
        's5_c_im': nrm(ks[19], (DEPTH, S5_GROUPS, S5_GROUP, S5_STATE)) * S5_STATE ** -0.5,
        's5_d': nrm(ks[20], (DEPTH, BRANCH_W)),
        's5_w_glu': nrm(ks[21], (DEPTH, BRANCH_W, BRANCH_W)) * BRANCH_W ** -0.5,
        's5_b_glu': 0.02 * nrm(ks[22], (DEPTH, BRANCH_W)),
        'final_norm': 1.0 + 0.02 * nrm(ks[23], (D_MODEL,)),
    }


def reference(x, norm_g, w_in, w_branch, w_out, gla_w_a2, gla_b_a, gla_norm, gdn_conv, gdn_a_log,
              gdn_dt_bias, gdn_norm, ret_norm, s5_lambda_re, s5_lambda_im, s5_log_dt, s5_b_re, s5_b_im,
              s5_c_re, s5_c_im, s5_d, s5_w_glu, s5_b_glu, final_norm):
    bsz, t, _ = x.shape
    for l in range(DEPTH):
        h = rms_norm(x, norm_g[l])
        p = h @ w_in[l]
        (qa, ka, va, lra, za, qkvb, ab, bb, zb, qc, kc, vc, zc, ud, zd, gates) = split_cols(p, IN_SIZES)
        ya = gla_branch(qa, ka, va, lra, za, gla_w_a2[l], gla_b_a[l], gla_norm[l])
        yb = gdn_branch(qkvb, ab, bb, zb, gdn_conv[l], gdn_a_log[l], gdn_dt_bias[l], gdn_norm[l])
        yc = retention_branch(qc, kc, vc, zc, ret_norm[l])
        yd = s5_branch(ud, zd, s5_lambda_re[l], s5_lambda_im[l], s5_log_dt[l], s5_b_re[l], s5_b_im[l],
                       s5_c_re[l], s5_c_im[l], s5_d[l], s5_w_glu[l], s5_b_glu[l])
        ys = jnp.stack([ya, yb, yc, yd], axis=2)
        proj = jnp.einsum('btnw,nwd->btnd', ys, w_branch[l])
        gate = jax.nn.sigmoid(gates.reshape(bsz, t, N_BRANCH, D_MODEL))
        merged = jnp.sum(gate * proj, axis=2)
        x = x + merged @ w_out[l]
    return rms_norm(x, final_norm)
```

```python
import functools
import math

import numpy as np
import jax
import jax.numpy as jnp
from jax import lax
from jax.experimental import pallas as pl
from jax.experimental.pallas import tpu as pltpu

F32 = jnp.float32
BF16 = jnp.bfloat16

D_MODEL = 1024
N_BRANCH = 4
BRANCH_W = 512
N_HEADS = 4
CHUNK = 64
EPS = 1e-6

GLA_DK = 64
GLA_DV = 128
GLA_RANK = 16
GLA_TAU = 16.0

GDN_DK = 128
GDN_DV = 128
CONV_K = 4
GDN_QKV = 2 * N_HEADS * GDN_DK + N_HEADS * GDN_DV

RET_DK = 64
RET_DV = 128
ROPE_BASE = 10000.0

S5_GROUP = 16
S5_GROUPS = 32
S5_STATE = 64
S5_SUB = 8
S5_NSTATE = 2 * S5_GROUPS * S5_STATE
S5_HALF = S5_NSTATE // 2

LANES = 128
SMALL_W = LANES
GDN_A_OFF = GLA_RANK
GDN_B_OFF = GLA_RANK + N_HEADS

SEG_A = 2 * N_HEADS * GLA_DK + 2 * BRANCH_W
SEG_B = GDN_QKV + BRANCH_W
SEG_C = 2 * N_HEADS * RET_DK + 2 * BRANCH_W
SEG_D = 2 * BRANCH_W
SEG_G = N_BRANCH * D_MODEL
SEG_WIDTHS = (SEG_A, SEG_B, SEG_C, SEG_D, SEG_G, SMALL_W)
D_PACKED = sum(SEG_WIDTHS)

VMEM_LIMIT_BYTES = 56 * 1024 * 1024

ROW_TILE = 256
TIME_TILE = 256


def _sigmoid(x):
    return 1.0 / (1.0 + jnp.exp(-x))


def _silu(x):
    return x * _sigmoid(x)


def _softplus(x):
    return jnp.maximum(x, 0.0) + jnp.log1p(jnp.exp(-jnp.abs(x)))


def _log_sigmoid(x):
    return -_softplus(-x)


def _bdot(a, b):
    return jnp.dot(a.astype(BF16), b.astype(BF16), preferred_element_type=F32)


def _bdot_nt(a, b):
    return lax.dot_general(a.astype(BF16), b.astype(BF16), (((1,), (1,)), ((), ())),
                           preferred_element_type=F32)


def _bdot_tn(a, b):
    return lax.dot_general(a.astype(BF16), b.astype(BF16), (((0,), (0,)), ((), ())),
                           preferred_element_type=F32)


def _split3(x):
    hi = x.astype(BF16)
    r1 = x - hi.astype(F32)
    mid = r1.astype(BF16)
    lo = (r1 - mid.astype(F32)).astype(BF16)
    return hi, mid, lo


def _exact_left_dot(m_bf16, x):
    hi, mid, lo = _split3(x)
    dot = functools.partial(jnp.dot, preferred_element_type=F32)
    return dot(m_bf16, hi) + dot(m_bf16, mid) + dot(m_bf16, lo)


def _iota2(shape, dim):
    return lax.broadcasted_iota(jnp.int32, shape, dim)


def _tri_incl(n):
    return (_iota2((n, n), 0) >= _iota2((n, n), 1)).astype(BF16)


def _head_norm(o, kind):
    if kind == "group":
        o = o - jnp.mean(o, axis=-1, keepdims=True)
    return o * lax.rsqrt(jnp.mean(o * o, axis=-1, keepdims=True) + EPS)


def _inproj_kernel(x_ref, g_ref, w_ref, *out_refs):
    x = x_ref[...]
    h = x * lax.rsqrt(jnp.mean(x * x, axis=-1, keepdims=True) + EPS) * g_ref[...]
    h = h.astype(BF16)
    off = 0
    for ref in out_refs:
        width = ref.shape[1]
        for c0 in range(0, width, 512):
            cw = min(512, width - c0)
            ref[:, c0:c0 + cw] = jnp.dot(h, w_ref[:, off + c0:off + c0 + cw],
                                         preferred_element_type=F32).astype(ref.dtype)
        off += width


def _inproj(x2d, g, w_packed):
    n = x2d.shape[0]
    tm = min(ROW_TILE, n)
    out_shape = tuple(jax.ShapeDtypeStruct((n, w), F32) for w in SEG_WIDTHS)
    return pl.pallas_call(
        _inproj_kernel,
        grid=(n // tm,),
        in_specs=[
            pl.BlockSpec((tm, D_MODEL), lambda i: (i, 0)),
            pl.BlockSpec((1, D_MODEL), lambda i: (0, 0)),
            pl.BlockSpec((D_MODEL, D_PACKED), lambda i: (0, 0), pipeline_mode=pl.Buffered(1)),
        ],
        out_specs=tuple(pl.BlockSpec((tm, w), lambda i: (i, 0)) for w in SEG_WIDTHS),
        out_shape=out_shape,
        compiler_params=pltpu.CompilerParams(dimension_semantics=("arbitrary",),
                                             vmem_limit_bytes=VMEM_LIMIT_BYTES),
        name="inproj",
    )(x2d, g, w_packed)


def _pack_w_in(w_in_l):
    sizes = (N_HEADS * GLA_DK, N_HEADS * GLA_DK, BRANCH_W, GLA_RANK, BRANCH_W,
             GDN_QKV, N_HEADS, N_HEADS, BRANCH_W,
             N_HEADS * RET_DK, N_HEADS * RET_DK, BRANCH_W, BRANCH_W,
             BRANCH_W, BRANCH_W, N_BRANCH * D_MODEL)
    offs = np.concatenate([[0], np.cumsum(sizes)])
    (qa, ka, va, lra, za, qkvb, ab, bb, zb, qc, kc, vc, zc, ud, zd, gates) = [
        w_in_l[:, int(offs[i]):int(offs[i + 1])] for i in range(len(sizes))]
    pad = jnp.zeros((D_MODEL, SMALL_W - GLA_RANK - 2 * N_HEADS), w_in_l.dtype)
    packed = jnp.concatenate([qa, ka, va, za, qkvb, zb, qc, kc, vc, zc, ud, zd, gates,
                              lra, ab, bb, pad], axis=1)
    return packed.astype(BF16)


def _linattn_chunk(q_e, k_e, k_s, v, dec_row, st_ref, dk):
    hk = N_HEADS * dk
    hv = v.shape[1]
    dv = hv // N_HEADS
    kbd = jnp.concatenate([k_e] * N_HEADS, axis=0)
    kmask = (_iota2((N_HEADS * CHUNK, hk), 0) // CHUNK) == (_iota2((N_HEADS * CHUNK, hk), 1) // dk)
    kbd = jnp.where(kmask, kbd, 0.0)
    scores = _bdot_nt(q_e, kbd)
    causal = _iota2((CHUNK, N_HEADS * CHUNK), 0) >= (_iota2((CHUNK, N_HEADS * CHUNK), 1) % CHUNK)
    scores = jnp.where(causal, scores, 0.0)
    vbd = jnp.concatenate([v] * N_HEADS, axis=0)
    vmask = (_iota2((N_HEADS * CHUNK, hv), 0) // CHUNK) == (_iota2((N_HEADS * CHUNK, hv), 1) // dv)
    vbd = jnp.where(vmask, vbd, 0.0)
    st = st_ref[...]
    o = _bdot(scores, vbd) + _bdot_nt(q_e, st)
    kv_t = _bdot_tn(v, k_s)
    smask = (_iota2((hv, hk), 0) // dv) == (_iota2((hv, hk), 1) // dk)
    st_ref[...] = st * dec_row + jnp.where(smask, kv_t, 0.0)
    return o


def _gla_kernel(p_ref, s_ref, wa2_ref, ba_ref, gn_ref, y_ref, st_ref):
    @pl.when(pl.program_id(1) == 0)
    def _():
        st_ref[...] = jnp.zeros_like(st_ref)

    hk = N_HEADS * GLA_DK
    tri = _tri_incl(CHUNK)
    for c in range(p_ref.shape[0] // CHUNK):
        rows = pl.ds(c * CHUNK, CHUNK)
        q = p_ref[rows, 0:hk] * (GLA_DK ** -0.5)
        k = p_ref[rows, hk:2 * hk]
        v = p_ref[rows, 2 * hk:2 * hk + BRANCH_W]
        z = p_ref[rows, 2 * hk + BRANCH_W:2 * hk + 2 * BRANCH_W]
        lr = s_ref[rows, 0:GLA_RANK]
        logits = _bdot(lr, wa2_ref[...]) + ba_ref[...]
        log_a = _log_sigmoid(logits) * (1.0 / GLA_TAU)
        cum = _exact_left_dot(tri, log_a)
        last = cum[CHUNK - 1:CHUNK, :]
        o = _linattn_chunk(q * jnp.exp(cum), k * jnp.exp(-cum), k * jnp.exp(last - cum), v,
                           jnp.exp(last), st_ref, GLA_DK)
        for h in range(N_HEADS):
            hs = slice(h * GLA_DV, (h + 1) * GLA_DV)
            y = _head_norm(o[:, hs], "rms") * gn_ref[...] * _silu(z[:, hs])
            y_ref[rows, hs] = y.astype(y_ref.dtype)


def _ret_kernel(p_ref, cos_ref, sin_ref, qe_ref, ke_ref, ks_ref, dec_ref, gn_ref, y_ref, st_ref):
    @pl.when(pl.program_id(1) == 0)
    def _():
        st_ref[...] = jnp.zeros_like(st_ref)

    hk = N_HEADS * RET_DK
    half = RET_DK // 2
    first_half = (_iota2((CHUNK, hk), 1) % RET_DK) < half

    def rotary(x, cos, sin):
        swapped = jnp.where(first_half, pltpu.roll(x, hk - half, 1), pltpu.roll(x, half, 1))
        return x * cos + swapped * sin

    for c in range(p_ref.shape[0] // CHUNK):
        rows = pl.ds(c * CHUNK, CHUNK)
        cos = cos_ref[rows, :]
        sin = sin_ref[rows, :]
        q = rotary(p_ref[rows, 0:hk], cos, sin)
        k = rotary(p_ref[rows, hk:2 * hk], cos, sin) * (RET_DK ** -0.5)
        v = p_ref[rows, 2 * hk:2 * hk + BRANCH_W]
        z = p_ref[rows, 2 * hk + BRANCH_W:2 * hk + 2 * BRANCH_W]
        o = _linattn_chunk(q * qe_ref[...], k * ke_ref[...], k * ks_ref[...], v,
                           dec_ref[...], st_ref, RET_DK)
        for h in range(N_HEADS):
            hs = slice(h * RET_DV, (h + 1) * RET_DV)
            y = _head_norm(o[:, hs], "group") * gn_ref[...] * _silu(z[:, hs])
            y_ref[rows, hs] = y.astype(y_ref.dtype)


def _mixer_call(kernel, ins, in_specs, scratch, bsz, t, name):
    tt = min(TIME_TILE, t)
    return pl.pallas_call(
        kernel,
        grid=(bsz, t // tt),
        in_specs=in_specs,
        out_specs=pl.BlockSpec((None, tt, BRANCH_W), lambda b, i: (b, i, 0)),
        out_shape=jax.ShapeDtypeStruct((bsz, t, BRANCH_W), BF16),
        scratch_shapes=scratch,
        compiler_params=pltpu.CompilerParams(dimension_semantics=("arbitrary", "arbitrary"),
                                             vmem_limit_bytes=VMEM_LIMIT_BYTES),
        name=name,
    )(*ins)


def _tok_spec(tt, width):
    return pl.BlockSpec((None, tt, width), lambda b, i: (b, i, 0))


def _const_spec(shape):
    return pl.BlockSpec(shape, lambda b, i: (0,) * len(shape))


def _gla(pa, ps, w_a2, b_a, g_norm):
    bsz, t, _ = pa.shape
    tt = min(TIME_TILE, t)
    hk = N_HEADS * GLA_DK
    ins = (pa, ps, w_a2.astype(BF16), b_a.reshape(1, hk), g_norm.reshape(1, GLA_DV))
    in_specs = [_tok_spec(tt, SEG_A), _tok_spec(tt, SMALL_W), _const_spec((GLA_RANK, hk)),
                _const_spec((1, hk)), _const_spec((1, GLA_DV))]
    scratch = [pltpu.VMEM((BRANCH_W, hk), F32)]
    return _mixer_call(_gla_kernel, ins, in_specs, scratch, bsz, t, "gla")


def _ret_tables(t):
    half = RET_DK // 2
    inv_freq = ROPE_BASE ** (-jnp.arange(half, dtype=F32) / half)
    ang = jnp.arange(t, dtype=F32)[:, None] * inv_freq
    cos, sin = jnp.cos(ang), jnp.sin(ang)
    cos_t = jnp.tile(jnp.concatenate([cos, cos], axis=1), (1, N_HEADS))
    sin_t = jnp.tile(jnp.concatenate([-sin, sin], axis=1), (1, N_HEADS))
    log_gamma = jnp.log1p(-jnp.exp2(-5.0 - jnp.arange(N_HEADS, dtype=F32)))
    lg = jnp.repeat(log_gamma, RET_DK)[None, :]
    cum = jnp.arange(1, CHUNK + 1, dtype=F32)[:, None] * lg
    last = cum[-1:, :]
    return cos_t, sin_t, jnp.exp(cum), jnp.exp(-cum), jnp.exp(last - cum), jnp.exp(last)


def _ret(pc, tables, g_norm):
    bsz, t, _ = pc.shape
    tt = min(TIME_TILE, t)
    hk = N_HEADS * RET_DK
    cos_t, sin_t, qe, ke, ks, dec = tables
    ins = (pc, cos_t, sin_t, qe, ke, ks, dec, g_norm.reshape(1, RET_DV))
    pos_spec = pl.BlockSpec((tt, hk), lambda b, i: (i, 0))
    in_specs = [_tok_spec(tt, SEG_C), pos_spec, pos_spec, _const_spec((CHUNK, hk)),
                _const_spec((CHUNK, hk)), _const_spec((CHUNK, hk)), _const_spec((1, hk)),
                _const_spec((1, RET_DV))]
    scratch = [pltpu.VMEM((BRANCH_W, hk), F32)]
    return _mixer_call(_ret_kernel, ins, in_specs, scratch, bsz, t, "retention")


CONV_PAD = 8


def _gdn_kernel(p_ref, s_ref, cw_ref, alog_ref, dtb_ref, gn_ref, y_ref, xbuf_ref, st_ref):
    tt = p_ref.shape[0]

    @pl.when(pl.program_id(1) == 0)
    def _():
        st_ref[...] = jnp.zeros_like(st_ref)
        xbuf_ref[0:CONV_PAD, :] = jnp.zeros((CONV_PAD, GDN_QKV), F32)

    xbuf_ref[CONV_PAD:CONV_PAD + tt, :] = p_ref[:, 0:GDN_QKV]
    conv = jnp.zeros((tt, GDN_QKV), F32)
    for j in range(CONV_K):
        conv = conv + cw_ref[j:j + 1, :] * xbuf_ref[pl.ds(CONV_PAD - (CONV_K - 1) + j, tt), :]
    xbuf_ref[0:CONV_PAD, :] = xbuf_ref[tt:tt + CONV_PAD, :]
    qkv = _silu(conv)

    tri = _tri_incl(CHUNK)
    ones = jnp.ones((CHUNK, CHUNK), BF16)
    r_i = _iota2((CHUNK, CHUNK), 0)
    c_i = _iota2((CHUNK, CHUNK), 1)
    incl = r_i >= c_i
    strict = r_i > c_i
    eye = (r_i == c_i).astype(F32)
    hq = N_HEADS * GDN_DK

    for h in range(N_HEADS):
        q_all = qkv[:, h * GDN_DK:(h + 1) * GDN_DK]
        k_all = qkv[:, hq + h * GDN_DK:hq + (h + 1) * GDN_DK]
        v_all = qkv[:, 2 * hq + h * GDN_DV:2 * hq + (h + 1) * GDN_DV]
        q_all = q_all * lax.rsqrt(jnp.sum(q_all * q_all, axis=-1, keepdims=True) + EPS) * (GDN_DK ** -0.5)
        k_all = k_all * lax.rsqrt(jnp.sum(k_all * k_all, axis=-1, keepdims=True) + EPS)
        a_col = s_ref[:, GDN_A_OFF + h:GDN_A_OFF + h + 1]
        b_col = s_ref[:, GDN_B_OFF + h:GDN_B_OFF + h + 1]
        g_all = -jnp.exp(alog_ref[:, h:h + 1]) * _softplus(a_col + dtb_ref[:, h:h + 1])
        beta_all = _sigmoid(b_col)
        g_b = jnp.broadcast_to(g_all, (tt, GDN_DK))
        for c in range(tt // CHUNK):
            rs = slice(c * CHUNK, (c + 1) * CHUNK)
            q, k, v = q_all[rs], k_all[rs], v_all[rs]
            beta = beta_all[rs]
            gb = g_b[rs]
            gcol = _exact_left_dot(tri, gb)
            gcol64 = gcol[:, 0:CHUNK]
            grow = _exact_left_dot(ones, jnp.where(r_i <= c_i, gb[:, 0:CHUNK], 0.0))
            decay = jnp.exp(jnp.where(incl, gcol64 - grow, -jnp.inf))
            glast = gcol[CHUNK - 1:CHUNK, :]
            egc = jnp.exp(gcol)
            kb = k * beta
            low = jnp.where(strict, _bdot_nt(kb, k) * decay, 0.0)
            m = -low
            inv = eye + m
            for _ in range(int(math.log2(CHUNK)) - 1):
                m = _bdot(m, m)
                inv = inv + _bdot(inv, m)
            uw = _bdot(inv, jnp.concatenate([v * beta, kb * egc], axis=1))
            u, w = uw[:, 0:GDN_DV], uw[:, GDN_DV:]
            attn = _bdot_nt(q, k) * decay
            st = st_ref[h]
            v_new = u - _bdot_nt(w, st)
            o = _bdot_nt(q * egc, st) + _bdot(attn, v_new)
            k_s = k * jnp.exp(glast - gcol)
            st_ref[h] = st * jnp.exp(glast) + _bdot_tn(v_new, k_s)
            z = p_ref[rs, GDN_QKV + h * GDN_DV:GDN_QKV + (h + 1) * GDN_DV]
            y = _head_norm(o, "rms") * gn_ref[...] * _silu(z)
            y_ref[rs, h * GDN_DV:(h + 1) * GDN_DV] = y.astype(y_ref.dtype)


def _gdn(pb, ps, conv_w, a_log, dt_bias, g_norm):
    bsz, t, _ = pb.shape
    tt = min(TIME_TILE, t)
    ins = (pb, ps, conv_w, a_log.reshape(1, N_HEADS), dt_bias.reshape(1, N_HEADS),
           g_norm.reshape(1, GDN_DV))
    in_specs = [_tok_spec(tt, SEG_B), _tok_spec(tt, SMALL_W), _const_spec((CONV_K, GDN_QKV)),
                _const_spec((1, N_HEADS)), _const_spec((1, N_HEADS)), _const_spec((1, GDN_DV))]
    scratch = [pltpu.VMEM((tt + CONV_PAD, GDN_QKV), F32),
               pltpu.VMEM((N_HEADS, GDN_DV, GDN_DK), F32)]
    return _mixer_call(_gdn_kernel, ins, in_specs, scratch, bsz, t, "gdn")


def _s5_tables(lam_re, lam_im, log_dt, b_re, b_im, c_re, c_im, tt):
    g, p = S5_GROUPS, S5_STATE
    dt = jnp.exp(log_dt)[:, None]

    def a_pow(n):
        n = jnp.asarray(n, F32)[..., None, None]
        mag = jnp.exp(lam_re * dt * n)
        return mag * jnp.cos(lam_im * dt * n), mag * jnp.sin(lam_im * dt * n)

    abar_re, abar_im = a_pow(1.0)
    nr, ni = abar_re - 1.0, abar_im
    den = jnp.square(lam_re) + jnp.square(lam_im)
    coef_re = (nr * lam_re + ni * lam_im) / den
    coef_im = (ni * lam_re - nr * lam_im) / den
    bbar_re = coef_re[..., None] * b_re - coef_im[..., None] * b_im
    bbar_im = coef_re[..., None] * b_im + coef_im[..., None] * b_re

    def lanes(re, im):
        lead = re.shape[:-2]
        hg = g // 2
        parts = [re[..., :hg, :].reshape(lead + (hg * p,)), im[..., :hg, :].reshape(lead + (hg * p,)),
                 re[..., hg:, :].reshape(lead + (hg * p,)), im[..., hg:, :].reshape(lead + (hg * p,))]
        return jnp.concatenate(parts, axis=-1)

    eye_g = jnp.eye(g, dtype=F32)
    bm_re = jnp.einsum("gpj,gh->gjhp", bbar_re, eye_g).reshape(BRANCH_W, g, p)
    bm_im = jnp.einsum("gpj,gh->gjhp", bbar_im, eye_g).reshape(BRANCH_W, g, p)
    b_mat = lanes(bm_re, bm_im)
    cm_re = jnp.einsum("gip,gh->hpgi", c_re, eye_g).reshape(g, p, BRANCH_W)
    cm_im = jnp.einsum("gip,gh->hpgi", -c_im, eye_g).reshape(g, p, BRANCH_W)
    c_mat = lanes(jnp.moveaxis(cm_re, 2, 0), jnp.moveaxis(cm_im, 2, 0)).T
    pr, pi = a_pow(jnp.arange(S5_SUB, dtype=F32))
    cb_re = (jnp.einsum("gip,tgp,gpj->tgij", c_re, pr, bbar_re) - jnp.einsum("gip,tgp,gpj->tgij", c_re, pi, bbar_im)
             - jnp.einsum("gip,tgp,gpj->tgij", c_im, pr, bbar_im) - jnp.einsum("gip,tgp,gpj->tgij", c_im, pi, bbar_re))
    taps = jnp.einsum("tgij,gh->tgjhi", cb_re, eye_g).reshape(S5_SUB, BRANCH_W, BRANCH_W)
    rho = jnp.arange(tt) % S5_SUB
    w_in = lanes(*a_pow((S5_SUB - 1 - rho).astype(F32)))
    w_out = lanes(*a_pow((rho + 1).astype(F32)))
    nblk = tt // S5_SUB
    steps = int(math.log2(nblk))
    scan_pw = lanes(*a_pow(S5_SUB * (2.0 ** jnp.arange(steps, dtype=F32))))
    carry_pw = lanes(*a_pow(S5_SUB * jnp.arange(1, nblk + 1, dtype=F32)))
    return (b_mat.astype(BF16), c_mat.astype(BF16), taps.astype(BF16), w_in, w_out, scan_pw, carry_pw)


def _cmul(a_re, a_im, b_re, b_im):
    return a_re * b_re - a_im * b_im, a_re * b_im + a_im * b_re


def _halves(x):
    q = S5_HALF // 2
    return [(x[:, 0:q], x[:, q:2 * q]), (x[:, 2 * q:3 * q], x[:, 3 * q:4 * q])]


def _cscale(x, w):
    out = []
    for (xr, xi), (wr, wi) in zip(_halves(x), _halves(w)):
        out.extend(_cmul(xr, xi, wr, wi))
    return jnp.concatenate(out, axis=1)


def _s5_kernel(p_ref, bm_ref, cm_ref, taps_ref, win_ref, wout_ref, spw_ref, cpw_ref, d_ref,
               wglu_ref, bglu_ref, y_ref, carry_ref):
    tt = p_ref.shape[0]
    nblk = tt // S5_SUB
    hw = BRANCH_W // 2

    @pl.when(pl.program_id(1) == 0)
    def _():
        carry_ref[...] = jnp.zeros_like(carry_ref)

    u = p_ref[:, 0:BRANCH_W]
    z = p_ref[:, BRANCH_W:2 * BRANCH_W]
    ub = u.astype(BF16)

    rho = _iota2((tt, BRANCH_W), 0) % S5_SUB
    y_lo = jnp.zeros((tt, hw), F32)
    y_hi = jnp.zeros((tt, hw), F32)
    for tau in range(S5_SUB):
        us = u if tau == 0 else jnp.where(rho >= tau, pltpu.roll(u, tau, 0), 0.0)
        us = us.astype(BF16)
        y_lo = y_lo + jnp.dot(us[:, 0:hw], taps_ref[tau, 0:hw, 0:hw], preferred_element_type=F32)
        y_hi = y_hi + jnp.dot(us[:, hw:], taps_ref[tau, hw:, hw:], preferred_element_type=F32)

    bu = jnp.concatenate(
        [jnp.dot(ub[:, 0:hw], bm_ref[0:hw, 0:S5_HALF], preferred_element_type=F32),
         jnp.dot(ub[:, hw:], bm_ref[hw:, S5_HALF:], preferred_element_type=F32)], axis=1)
    zw = _cscale(bu, win_ref[...])
    pool = (_iota2((nblk, tt), 1) // S5_SUB == _iota2((nblk, tt), 0)).astype(BF16)
    s = _exact_left_dot(pool, zw)

    row = _iota2((nblk, S5_NSTATE), 0)
    x = s
    for kk in range(int(math.log2(nblk))):
        sh = 1 << kk
        shifted = jnp.where(row >= sh, pltpu.roll(x, sh, 0), 0.0)
        x = x + _cscale(shifted, spw_ref[kk:kk + 1, :])
    carry = carry_ref[...]
    x = x + _cscale(jnp.broadcast_to(carry, (nblk, S5_NSTATE)), cpw_ref[...])
    carry_ref[...] = x[nblk - 1:nblk, :]
    x_prev = jnp.where(row >= 1, pltpu.roll(x, 1, 0), jnp.broadcast_to(carry, (nblk, S5_NSTATE)))

    expand = (_iota2((tt, nblk), 0) // S5_SUB == _iota2((tt, nblk), 1)).astype(BF16)
    xe = _exact_left_dot(expand, x_prev)
    zc = _cscale(xe, wout_ref[...]).astype(BF16)
    y_lo = y_lo + jnp.dot(zc[:, 0:S5_HALF], cm_ref[0:S5_HALF, 0:hw], preferred_element_type=F32)
    y_hi = y_hi + jnp.dot(zc[:, S5_HALF:], cm_ref[S5_HALF:, hw:], preferred_element_type=F32)

    y = jnp.concatenate([y_lo, y_hi], axis=1) + d_ref[...] * u
    y = 0.5 * y * (1.0 + jnp.tanh(math.sqrt(2.0 / math.pi) * (y + 0.044715 * (y * y * y))))
    y = y * _sigmoid(_bdot(y, wglu_ref[...]) + bglu_ref[...])
    y_ref[...] = (y * _silu(z)).astype(y_ref.dtype)


def _s5(pd, tables, d, w_glu, b_glu):
    bsz, t, _ = pd.shape
    tt = min(TIME_TILE, t)
    nblk = tt // S5_SUB
    steps = int(math.log2(nblk))
    b_mat, c_mat, taps, w_in, w_out, scan_pw, carry_pw = tables
    ins = (pd, b_mat, c_mat, taps, w_in, w_out, scan_pw, carry_pw, d.reshape(1, BRANCH_W),
           w_glu.astype(BF16), b_glu.reshape(1, BRANCH_W))
    in_specs = [_tok_spec(tt, SEG_D), _const_spec((BRANCH_W, S5_NSTATE)), _const_spec((S5_NSTATE, BRANCH_W)),
                _const_spec((S5_SUB, BRANCH_W, BRANCH_W)), _const_spec((tt, S5_NSTATE)),
                _const_spec((tt, S5_NSTATE)), _const_spec((steps, S5_NSTATE)),
                _const_spec((nblk, S5_NSTATE)), _const_spec((1, BRANCH_W)),
                _const_spec((BRANCH_W, BRANCH_W)), _const_spec((1, BRANCH_W))]
    scratch = [pltpu.VMEM((1, S5_NSTATE), F32)]
    return _mixer_call(_s5_kernel, ins, in_specs, scratch, bsz, t, "s5")


def _merge_kernel(x_ref, ya_ref, yb_ref, yc_ref, yd_ref, g_ref, wb_ref, wo_ref, fn_ref, o_ref, *,
                  final_norm):
    merged = jnp.zeros((x_ref.shape[0], D_MODEL), F32)
    for n, y_ref in enumerate((ya_ref, yb_ref, yc_ref, yd_ref)):
        proj = jnp.dot(y_ref[...], wb_ref[n], preferred_element_type=F32)
        merged = merged + _sigmoid(g_ref[:, n * D_MODEL:(n + 1) * D_MODEL]) * proj
    out = x_ref[...] + _bdot(merged, wo_ref[...])
    if final_norm:
        out = out * lax.rsqrt(jnp.mean(out * out, axis=-1, keepdims=True) + EPS) * fn_ref[...]
    o_ref[...] = out


def _merge(x2d, ys, gates, w_branch, w_out, fnorm, final_norm):
    n = x2d.shape[0]
    tm = min(ROW_TILE, n)
    row = lambda w: pl.BlockSpec((tm, w), lambda i: (i, 0))
    return pl.pallas_call(
        functools.partial(_merge_kernel, final_norm=final_norm),
        grid=(n // tm,),
        in_specs=[row(D_MODEL), row(BRANCH_W), row(BRANCH_W), row(BRANCH_W), row(BRANCH_W),
                  row(SEG_G),
                  pl.BlockSpec((N_BRANCH, BRANCH_W, D_MODEL), lambda i: (0, 0, 0)),
                  pl.BlockSpec((D_MODEL, D_MODEL), lambda i: (0, 0)),
                  pl.BlockSpec((1, D_MODEL), lambda i: (0, 0))],
        out_specs=row(D_MODEL),
        out_shape=jax.ShapeDtypeStruct((n, D_MODEL), F32),
        compiler_params=pltpu.CompilerParams(dimension_semantics=("arbitrary",),
                                             vmem_limit_bytes=VMEM_LIMIT_BYTES),
        name="merge",
    )(x2d, *ys, gates, w_branch.astype(BF16), w_out.astype(BF16), fnorm.reshape(1, D_MODEL))


def kernel(x, norm_g, w_in, w_branch, w_out, gla_w_a2, gla_b_a, gla_norm, gdn_conv, gdn_a_log,
           gdn_dt_bias, gdn_norm, ret_norm, s5_lambda_re, s5_lambda_im, s5_log_dt, s5_b_re, s5_b_im,
           s5_c_re, s5_c_im, s5_d, s5_w_glu, s5_b_glu, final_norm):
    bsz, t, _ = x.shape
    depth = w_in.shape[0]
    n = bsz * t
    tt = min(TIME_TILE, t)
    ret_tables = _ret_tables(t)
    x2d = x.reshape(n, D_MODEL)
    for l in range(depth):
        pa, pb, pc, pd, pg, ps = _inproj(x2d, norm_g[l].reshape(1, D_MODEL), _pack_w_in(w_in[l]))
        tok = lambda a: a.reshape(bsz, t, a.shape[-1])
        ps3 = tok(ps)
        ya = _gla(tok(pa), ps3, gla_w_a2[l], gla_b_a[l], gla_norm[l])
        yb = _gdn(tok(pb), ps3, gdn_conv[l], gdn_a_log[l], gdn_dt_bias[l], gdn_norm[l])
        yc = _ret(tok(pc), ret_tables, ret_norm[l])
        s5_tables = _s5_tables(s5_lambda_re[l], s5_lambda_im[l], s5_log_dt[l], s5_b_re[l], s5_b_im[l],
                               s5_c_re[l], s5_c_im[l], tt)
        yd = _s5(tok(pd), s5_tables, s5_d[l], s5_w_glu[l], s5_b_glu[l])
        ys = tuple(y.reshape(n, BRANCH_W) for y in (ya, yb, yc, yd))
        x2d = _merge(x2d, ys, pg, w_branch[l], w_out[l], final_norm, final_norm=(l == depth - 1))
    return x2d.reshape(bsz, t, D_MODEL)
```

```python
import functools
import math

import numpy as np
import jax
import jax.numpy as jnp
from jax import lax
from jax.experimental import pallas as pl
from jax.experimental.pallas import tpu as pltpu

F32 = jnp.float32
BF16 = jnp.bfloat16

D_MODEL = 1024
N_BRANCH = 4
BRANCH_W = 512
N_HEADS = 4
CHUNK = 64
EPS = 1e-6

GLA_DK = 64
GLA_DV = 128
GLA_RANK = 16
GLA_TAU = 16.0

GDN_DK = 128
GDN_DV = 128
CONV_K = 4
GDN_QKV = 2 * N_HEADS * GDN_DK + N_HEADS * GDN_DV

RET_DK = 64
RET_DV = 128
ROPE_BASE = 10000.0

S5_GROUP = 16
S5_GROUPS = 32
S5_STATE = 64
S5_SUB = 8
S5_NSTATE = 2 * S5_GROUPS * S5_STATE
S5_HALF = S5_NSTATE // 2

LANES = 128
SMALL_W = LANES
GDN_A_OFF = GLA_RANK
GDN_B_OFF = GLA_RANK + N_HEADS

SEG_A = 2 * N_HEADS * GLA_DK + 2 * BRANCH_W
SEG_B = GDN_QKV + BRANCH_W
SEG_C = 2 * N_HEADS * RET_DK + 2 * BRANCH_W
SEG_D = 2 * BRANCH_W
SEG_G = N_BRANCH * D_MODEL
SEG_WIDTHS = (SEG_A, SEG_B, SEG_C, SEG_D, SEG_G, SMALL_W)
D_PACKED = sum(SEG_WIDTHS)

VMEM_LIMIT_BYTES = 56 * 1024 * 1024

ROW_TILE = 256
TIME_TILE = 256


def _sigmoid(x):
    return 1.0 / (1.0 + jnp.exp(-x))


def _silu(x):
    return x * _sigmoid(x)


def _softplus(x):
    return jnp.maximum(x, 0.0) + jnp.log1p(jnp.exp(-jnp.abs(x)))


def _log_sigmoid(x):
    return -_softplus(-x)


def _bdot(a, b):
    return jnp.dot(a.astype(BF16), b.astype(BF16), preferred_element_type=F32)


def _bdot_nt(a, b):
    return lax.dot_general(a.astype(BF16), b.astype(BF16), (((1,), (1,)), ((), ())),
                           preferred_element_type=F32)


def _bdot_tn(a, b):
    return lax.dot_general(a.astype(BF16), b.astype(BF16), (((0,), (0,)), ((), ())),
                           preferred_element_type=F32)


def _split3(x):
    hi = x.astype(BF16)
    r1 = x - hi.astype(F32)
    mid = r1.astype(BF16)
    lo = (r1 - mid.astype(F32)).astype(BF16)
    return hi, mid, lo


def _exact_left_dot(m_bf16, x):
    hi, mid, lo = _split3(x)
    dot = functools.partial(jnp.dot, preferred_element_type=F32)
    return dot(m_bf16, hi) + dot(m_bf16, mid) + dot(m_bf16, lo)


def _iota2(shape, dim):
    return lax.broadcasted_iota(jnp.int32, shape, dim)


def _tri_incl(n):
    return (_iota2((n, n), 0) >= _iota2((n, n), 1)).astype(BF16)


def _head_norm(o, kind):
    if kind == "group":
        o = o - jnp.mean(o, axis=-1, keepdims=True)
    return o * lax.rsqrt(jnp.mean(o * o, axis=-1, keepdims=True) + EPS)


def _inproj_kernel(x_ref, g_ref, w_ref, *out_refs):
    x = x_ref[...]
    h = x * lax.rsqrt(jnp.mean(x * x, axis=-1, keepdims=True) + EPS) * g_ref[...]
    h = h.astype(BF16)
    off = 0
    for ref in out_refs:
        width = ref.shape[1]
        for c0 in range(0, width, 512):
            cw = min(512, width - c0)
            ref[:, c0:c0 + cw] = jnp.dot(h, w_ref[:, off + c0:off + c0 + cw],
                                         preferred_element_type=F32).astype(ref.dtype)
        off += width


def _inproj(x2d, g, w_packed):
    n = x2d.shape[0]
    tm = min(ROW_TILE, n)
    out_shape = tuple(jax.ShapeDtypeStruct((n, w), F32) for w in SEG_WIDTHS)
    return pl.pallas_call(
        _inproj_kernel,
        grid=(n // tm,),
        in_specs=[
            pl.BlockSpec((tm, D_MODEL), lambda i: (i, 0)),
            pl.BlockSpec((1, D_MODEL), lambda i: (0, 0)),
            pl.BlockSpec((D_MODEL, D_PACKED), lambda i: (0, 0), pipeline_mode=pl.Buffered(1)),
        ],
        out_specs=tuple(pl.BlockSpec((tm, w), lambda i: (i, 0)) for w in SEG_WIDTHS),
        out_shape=out_shape,
        compiler_params=pltpu.CompilerParams(dimension_semantics=("arbitrary",),
                                             vmem_limit_bytes=VMEM_LIMIT_BYTES),
        name="inproj",
    )(x2d, g, w_packed)


def _pack_w_in(w_in_l):
    sizes = (N_HEADS * GLA_DK, N_HEADS * GLA_DK, BRANCH_W, GLA_RANK, BRANCH_W,
             GDN_QKV, N_HEADS, N_HEADS, BRANCH_W,
             N_HEADS * RET_DK, N_HEADS * RET_DK, BRANCH_W, BRANCH_W,
             BRANCH_W, BRANCH_W, N_BRANCH * D_MODEL)
    offs = np.concatenate([[0], np.cumsum(sizes)])
    (qa, ka, va, lra, za, qkvb, ab, bb, zb, qc, kc, vc, zc, ud, zd, gates) = [
        w_in_l[:, int(offs[i]):int(offs[i + 1])] for i in range(len(sizes))]
    pad = jnp.zeros((D_MODEL, SMALL_W - GLA_RANK - 2 * N_HEADS), w_in_l.dtype)
    packed = jnp.concatenate([qa, ka, va, za, qkvb, zb, qc, kc, vc, zc, ud, zd, gates,
                              lra, ab, bb, pad], axis=1)
    return packed.astype(BF16)


def _linattn_chunk(q_e, k_e, k_s, v, dec_row, st_ref, dk):
    hk = N_HEADS * dk
    hv = v.shape[1]
    dv = hv // N_HEADS
    kbd = jnp.concatenate([k_e] * N_HEADS, axis=0)
    kmask = (_iota2((N_HEADS * CHUNK, hk), 0) // CHUNK) == (_iota2((N_HEADS * CHUNK, hk), 1) // dk)
    kbd = jnp.where(kmask, kbd, 0.0)
    scores = _bdot_nt(q_e, kbd)
    causal = _iota2((CHUNK, N_HEADS * CHUNK), 0) >= (_iota2((CHUNK, N_HEADS * CHUNK), 1) % CHUNK)
    scores = jnp.where(causal, scores, 0.0)
    vbd = jnp.concatenate([v] * N_HEADS, axis=0)
    vmask = (_iota2((N_HEADS * CHUNK, hv), 0) // CHUNK) == (_iota2((N_HEADS * CHUNK, hv), 1) // dv)
    vbd = jnp.where(vmask, vbd, 0.0)
    st = st_ref[...]
    o = _bdot(scores, vbd) + _bdot_nt(q_e, st)
    kv_t = _bdot_tn(v, k_s)
    smask = (_iota2((hv, hk), 0) // dv) == (_iota2((hv, hk), 1) // dk)
    st_ref[...] = st * dec_row + jnp.where(smask, kv_t, 0.0)
    return o


def _gla_kernel(p_ref, s_ref, wa2_ref, ba_ref, gn_ref, y_ref, st_ref):
    @pl.when(pl.program_id(1) == 0)
    def _():
        st_ref[...] = jnp.zeros_like(st_ref)

    hk = N_HEADS * GLA_DK
    tri = _tri_incl(CHUNK)
    for c in range(p_ref.shape[0] // CHUNK):
        rows = pl.ds(c * CHUNK, CHUNK)
        q = p_ref[rows, 0:hk] * (GLA_DK ** -0.5)
        k = p_ref[rows, hk:2 * hk]
        v = p_ref[rows, 2 * hk:2 * hk + BRANCH_W]
        z = p_ref[rows, 2 * hk + BRANCH_W:2 * hk + 2 * BRANCH_W]
        lr = s_ref[rows, 0:GLA_RANK]
        logits = _bdot(lr, wa2_ref[...]) + ba_ref[...]
        log_a = _log_sigmoid(logits) * (1.0 / GLA_TAU)
        cum = _exact_left_dot(tri, log_a)
        last = cum[CHUNK - 1:CHUNK, :]
        o = _linattn_chunk(q * jnp.exp(cum), k * jnp.exp(-cum), k * jnp.exp(last - cum), v,
                           jnp.exp(last), st_ref, GLA_DK)
        for h in range(N_HEADS):
            hs = slice(h * GLA_DV, (h + 1) * GLA_DV)
            y = _head_norm(o[:, hs], "rms") * gn_ref[...] * _silu(z[:, hs])
            y_ref[rows, hs] = y.astype(y_ref.dtype)


def _ret_kernel(p_ref, cos_ref, sin_ref, qe_ref, ke_ref, ks_ref, dec_ref, gn_ref, y_ref, st_ref):
    @pl.when(pl.program_id(1) == 0)
    def _():
        st_ref[...] = jnp.zeros_like(st_ref)

    hk = N_HEADS * RET_DK
    half = RET_DK // 2
    first_half = (_iota2((CHUNK, hk), 1) % RET_DK) < half

    def rotary(x, cos, sin):
        swapped = jnp.where(first_half, pltpu.roll(x, hk - half, 1), pltpu.roll(x, half, 1))
        return x * cos + swapped * sin

    for c in range(p_ref.shape[0] // CHUNK):
        rows = pl.ds(c * CHUNK, CHUNK)
        cos = cos_ref[rows, :]
        sin = sin_ref[rows, :]
        q = rotary(p_ref[rows, 0:hk], cos, sin)
        k = rotary(p_ref[rows, hk:2 * hk], cos, sin) * (RET_DK ** -0.5)
        v = p_ref[rows, 2 * hk:2 * hk + BRANCH_W]
        z = p_ref[rows, 2 * hk + BRANCH_W:2 * hk + 2 * BRANCH_W]
        o = _linattn_chunk(q * qe_ref[...], k * ke_ref[...], k * ks_ref[...], v,
                           dec_ref[...], st_ref, RET_DK)
        for h in range(N_HEADS):
            hs = slice(h * RET_DV, (h + 1) * RET_DV)
            y = _head_norm(o[:, hs], "group") * gn_ref[...] * _silu(z[:, hs])
            y_ref[rows, hs] = y.astype(y_ref.dtype)


def _mixer_call(kernel, ins, in_specs, scratch, bsz, t, name):
    tt = min(TIME_TILE, t)
    return pl.pallas_call(
        kernel,
        grid=(bsz, t // tt),
        in_specs=in_specs,
        out_specs=pl.BlockSpec((None, tt, BRANCH_W), lambda b, i: (b, i, 0)),
        out_shape=jax.ShapeDtypeStruct((bsz, t, BRANCH_W), BF16),
        scratch_shapes=scratch,
        compiler_params=pltpu.CompilerParams(dimension_semantics=("arbitrary", "arbitrary"),
                                             vmem_limit_bytes=VMEM_LIMIT_BYTES),
        name=name,
    )(*ins)


def _tok_spec(tt, width):
    return pl.BlockSpec((None, tt, width), lambda b, i: (b, i, 0))


def _const_spec(shape):
    return pl.BlockSpec(shape, lambda b, i: (0,) * len(shape))


def _gla(pa, ps, w_a2, b_a, g_norm):
    bsz, t, _ = pa.shape
    tt = min(TIME_TILE, t)
    hk = N_HEADS * GLA_DK
    ins = (pa, ps, w_a2.astype(BF16), b_a.reshape(1, hk), g_norm.reshape(1, GLA_DV))
    in_specs = [_tok_spec(tt, SEG_A), _tok_spec(tt, SMALL_W), _const_spec((GLA_RANK, hk)),
                _const_spec((1, hk)), _const_spec((1, GLA_DV))]
    scratch = [pltpu.VMEM((BRANCH_W, hk), F32)]
    return _mixer_call(_gla_kernel, ins, in_specs, scratch, bsz, t, "gla")


def _ret_tables(t):
    half = RET_DK // 2
    inv_freq = ROPE_BASE ** (-jnp.arange(half, dtype=F32) / half)
    ang = jnp.arange(t, dtype=F32)[:, None] * inv_freq
    cos, sin = jnp.cos(ang), jnp.sin(ang)
    cos_t = jnp.tile(jnp.concatenate([cos, cos], axis=1), (1, N_HEADS))
    sin_t = jnp.tile(jnp.concatenate([-sin, sin], axis=1), (1, N_HEADS))
    log_gamma = jnp.log1p(-jnp.exp2(-5.0 - jnp.arange(N_HEADS, dtype=F32)))
    lg = jnp.repeat(log_gamma, RET_DK)[None, :]
    cum = jnp.arange(1, CHUNK + 1, dtype=F32)[:, None] * lg
    last = cum[-1:, :]
    return cos_t, sin_t, jnp.exp(cum), jnp.exp(-cum), jnp.exp(last - cum), jnp.exp(last)


def _ret(pc, tables, g_norm):
    bsz, t, _ = pc.shape
    tt = min(TIME_TILE, t)
    hk = N_HEADS * RET_DK
    cos_t, sin_t, qe, ke, ks, dec = tables
    ins = (pc, cos_t, sin_t, qe, ke, ks, dec, g_norm.reshape(1, RET_DV))
    pos_spec = pl.BlockSpec((tt, hk), lambda b, i: (i, 0))
    in_specs = [_tok_spec(tt, SEG_C), pos_spec, pos_spec, _const_spec((CHUNK, hk)),
                _const_spec((CHUNK, hk)), _const_spec((CHUNK, hk)), _const_spec((1, hk)),
                _const_spec((1, RET_DV))]
    scratch = [pltpu.VMEM((BRANCH_W, hk), F32)]
    return _mixer_call(_ret_kernel, ins, in_specs, scratch, bsz, t, "retention")


CONV_PAD = 8


GDN_TIME_TILE = 128
GDN_HD = N_HEADS * GDN_DK
GDN_PAIR = 2 * GDN_DK


def _bmm(a, b):
    return jnp.einsum("nck,nkj->ncj", a.astype(BF16), b.astype(BF16), preferred_element_type=F32)


def _bmm_nt(a, b):
    return jnp.einsum("nck,nsk->ncs", a.astype(BF16), b.astype(BF16), preferred_element_type=F32)


def _split2(x):
    hi = x.astype(BF16)
    return hi, (x - hi.astype(F32)).astype(BF16)


def _cumsum_rows(tri_b, x):
    hi, lo = _split2(x)
    return (jnp.einsum("nct,ntl->ncl", tri_b, hi, preferred_element_type=F32)
            + jnp.einsum("nct,ntl->ncl", tri_b, lo, preferred_element_type=F32))


def _head_lanes(x, lane0, width):
    n, rows = x.shape[0], x.shape[1]
    total = N_HEADS * width
    lane = lax.broadcasted_iota(jnp.int32, (1, 1, total), 2)
    out = jnp.broadcast_to(x[:, :, lane0:lane0 + 1], (n, rows, total))
    for h in range(1, N_HEADS):
        out = jnp.where(lane >= h * width, jnp.broadcast_to(x[:, :, lane0 + h:lane0 + h + 1], (n, rows, total)), out)
    return out


def _block_diag_rows(x, blk_cols):
    n, _, cols = x.shape
    rows = N_HEADS * CHUNK
    mask = (lax.broadcasted_iota(jnp.int32, (1, rows, cols), 1) // CHUNK
            == lax.broadcasted_iota(jnp.int32, (1, rows, cols), 2) // blk_cols)
    return jnp.where(mask, jnp.concatenate([x] * N_HEADS, axis=1), jnp.zeros((), x.dtype))


def _gdn_kernel(p_ref, s_ref, cw_ref, gpar_ref, gn_ref, y_ref, xbuf_ref, st_ref):
    nb, tt = p_ref.shape[0], p_ref.shape[1]
    nc = tt // CHUNK
    n = nc * nb

    @pl.when(pl.program_id(0) == 0)
    def _():
        st_ref[...] = jnp.zeros_like(st_ref)
        xbuf_ref[:, 0:CONV_PAD, :] = jnp.zeros((nb, CONV_PAD, GDN_QKV), F32)

    xbuf_ref[:, CONV_PAD:CONV_PAD + tt, :] = p_ref[:, :, 0:GDN_QKV]
    conv = jnp.zeros((nb, tt, GDN_QKV), F32)
    for j in range(CONV_K):
        conv = conv + cw_ref[j:j + 1, :] * xbuf_ref[:, pl.ds(CONV_PAD - (CONV_K - 1) + j, tt), :]
    xbuf_ref[:, 0:CONV_PAD, :] = xbuf_ref[:, tt:tt + CONV_PAD, :]
    qkv = _silu(conv)

    def chunks(x):
        return jnp.concatenate([x[:, c * CHUNK:(c + 1) * CHUNK, :] for c in range(nc)], axis=0)

    def l2n(x):
        parts = []
        for h in range(N_HEADS):
            xh = x[:, :, h * GDN_DK:(h + 1) * GDN_DK]
            parts.append(xh * lax.rsqrt(jnp.sum(xh * xh, axis=-1, keepdims=True) + EPS))
        return jnp.concatenate(parts, axis=-1)

    q = l2n(chunks(qkv[:, :, 0:GDN_HD])) * (GDN_DK ** -0.5)
    k = l2n(chunks(qkv[:, :, GDN_HD:2 * GDN_HD]))
    v = chunks(qkv[:, :, 2 * GDN_HD:3 * GDN_HD])
    z = chunks(p_ref[:, :, GDN_QKV:GDN_QKV + BRANCH_W])
    s = chunks(s_ref[...])

    g = -jnp.exp(gpar_ref[0:1, :]) * _softplus(s + gpar_ref[1:2, :])
    beta = _head_lanes(_sigmoid(s), GDN_B_OFF, GDN_DK)

    r64 = lax.broadcasted_iota(jnp.int32, (1, CHUNK, N_HEADS * CHUNK), 1)
    c64 = lax.broadcasted_iota(jnp.int32, (1, CHUNK, N_HEADS * CHUNK), 2) % CHUNK
    tri_b = jnp.broadcast_to(_tri_incl(CHUNK)[None], (n, CHUNK, CHUNK))
    dlog = _cumsum_rows(tri_b, jnp.where(r64 > c64, _head_lanes(g, GDN_A_OFF, CHUNK), 0.0))
    decay = jnp.where(r64 >= c64, jnp.exp(dlog), 0.0)
    gcol = _head_lanes(_cumsum_rows(tri_b, g), GDN_A_OFF, GDN_DK)
    glast = gcol[:, CHUNK - 1:CHUNK, :]
    egc = jnp.exp(gcol)
    dec = jnp.exp(glast)

    kb = k * beta
    q_e = q * egc
    k_st = jnp.swapaxes(k * jnp.exp(glast - gcol), 1, 2).astype(BF16)
    kq = _bmm_nt(jnp.concatenate([kb, q], axis=1), _block_diag_rows(k.astype(BF16), GDN_DK))
    low = jnp.where(r64 > c64, kq[:, 0:CHUNK, :] * decay, 0.0)
    attn = kq[:, CHUNK:, :] * decay

    m = -low
    inv = jnp.where(r64 == c64, 1.0, 0.0) + m
    levels = int(math.log2(CHUNK)) - 1
    for lvl in range(levels):
        mbd = _block_diag_rows(m.astype(BF16), CHUNK)
        if lvl == 0:
            m = _bmm(m, mbd)
        else:
            both = _bmm(jnp.concatenate([m, inv], axis=1), mbd)
            m = both[:, 0:CHUNK, :]
            inv = inv + both[:, CHUNK:, :]
    inv = inv + _bmm(inv, _block_diag_rows(m.astype(BF16), CHUNK))

    vb = v * beta
    kbe = kb * egc
    us, ws = [], []
    for h in range(N_HEADS):
        hs = slice(h * GDN_DV, (h + 1) * GDN_DV)
        uw = _bmm(inv[:, :, h * CHUNK:(h + 1) * CHUNK], jnp.concatenate([vb[:, :, hs], kbe[:, :, hs]], axis=-1))
        us.append(uw[:, :, 0:GDN_DV])
        ws.append(uw[:, :, GDN_DV:])
    u = jnp.concatenate(us, axis=-1)
    wq = jnp.concatenate([jnp.concatenate(ws, axis=-1), q_e], axis=1)

    pair_mask = (lax.broadcasted_iota(jnp.int32, (1, GDN_PAIR, GDN_PAIR), 1) // GDN_DK
                 == lax.broadcasted_iota(jnp.int32, (1, GDN_PAIR, GDN_PAIR), 2) // GDN_DV)
    for c in range(nc):
        bs = slice(c * nb, (c + 1) * nb)
        ws_parts = []
        for pr in range(N_HEADS // 2):
            ps_ = slice(pr * GDN_PAIR, (pr + 1) * GDN_PAIR)
            ws_parts.append(_bmm(wq[bs, :, ps_], st_ref[:, pr]))
        w_s = jnp.concatenate(ws_parts, axis=-1)
        v_new = u[bs] - w_s[:, 0:CHUNK, :]
        o = w_s[:, CHUNK:, :] + _bmm(attn[bs], _block_diag_rows(v_new.astype(BF16), GDN_DV))
        for pr in range(N_HEADS // 2):
            ps_ = slice(pr * GDN_PAIR, (pr + 1) * GDN_PAIR)
            kv = _bmm(k_st[bs, ps_, :], v_new[:, :, ps_])
            st_ref[:, pr] = st_ref[:, pr] * dec[bs, :, ps_] + jnp.where(pair_mask, kv, 0.0)
        rows = slice(c * CHUNK, (c + 1) * CHUNK)
        for h in range(N_HEADS):
            hs = slice(h * GDN_DV, (h + 1) * GDN_DV)
            y = _head_norm(o[:, :, hs], "rms") * gn_ref[...] * _silu(z[bs, :, hs])
            y_ref[:, rows, hs] = y.astype(y_ref.dtype)


def _gdn(pb, ps, conv_w, a_log, dt_bias, g_norm):
    bsz, t, _ = pb.shape
    tt = min(GDN_TIME_TILE, t)
    lane_pad = lambda x: jnp.pad(x.reshape(1, N_HEADS), ((0, 0), (GDN_A_OFF, SMALL_W - GDN_A_OFF - N_HEADS)))
    gpar = jnp.concatenate([lane_pad(a_log), lane_pad(dt_bias)], axis=0)
    ins = (pb, ps, conv_w, gpar, g_norm.reshape(1, GDN_DV))
    tok = lambda w: pl.BlockSpec((bsz, tt, w), lambda i: (0, i, 0))
    const = lambda shape: pl.BlockSpec(shape, lambda i: (0,) * len(shape))
    return pl.pallas_call(
        _gdn_kernel,
        grid=(t // tt,),
        in_specs=[tok(SEG_B), tok(SMALL_W), const((CONV_K, GDN_QKV)), const((2, SMALL_W)), const((1, GDN_DV))],
        out_specs=tok(BRANCH_W),
        out_shape=jax.ShapeDtypeStruct((bsz, t, BRANCH_W), BF16),
        scratch_shapes=[pltpu.VMEM((bsz, tt + CONV_PAD, GDN_QKV), F32),
                        pltpu.VMEM((bsz, N_HEADS // 2, GDN_PAIR, GDN_PAIR), F32)],
        compiler_params=pltpu.CompilerParams(dimension_semantics=("arbitrary",),
                                             vmem_limit_bytes=VMEM_LIMIT_BYTES),
        name="gdn",
    )(*ins)


def _s5_tables(lam_re, lam_im, log_dt, b_re, b_im, c_re, c_im, tt):
    g, p = S5_GROUPS, S5_STATE
    dt = jnp.exp(log_dt)[:, None]

    def a_pow(n):
        n = jnp.asarray(n, F32)[..., None, None]
        mag = jnp.exp(lam_re * dt * n)
        return mag * jnp.cos(lam_im * dt * n), mag * jnp.sin(lam_im * dt * n)

    abar_re, abar_im = a_pow(1.0)
    nr, ni = abar_re - 1.0, abar_im
    den = jnp.square(lam_re) + jnp.square(lam_im)
    coef_re = (nr * lam_re + ni * lam_im) / den
    coef_im = (ni * lam_re - nr * lam_im) / den
    bbar_re = coef_re[..., None] * b_re - coef_im[..., None] * b_im
    bbar_im = coef_re[..., None] * b_im + coef_im[..., None] * b_re

    def lanes(re, im):
        lead = re.shape[:-2]
        hg = g // 2
        parts = [re[..., :hg, :].reshape(lead + (hg * p,)), im[..., :hg, :].reshape(lead + (hg * p,)),
                 re[..., hg:, :].reshape(lead + (hg * p,)), im[..., hg:, :].reshape(lead + (hg * p,))]
        return jnp.concatenate(parts, axis=-1)

    eye_g = jnp.eye(g, dtype=F32)
    bm_re = jnp.einsum("gpj,gh->gjhp", bbar_re, eye_g).reshape(BRANCH_W, g, p)
    bm_im = jnp.einsum("gpj,gh->gjhp", bbar_im, eye_g).reshape(BRANCH_W, g, p)
    b_mat = lanes(bm_re, bm_im)
    cm_re = jnp.einsum("gip,gh->hpgi", c_re, eye_g).reshape(g, p, BRANCH_W)
    cm_im = jnp.einsum("gip,gh->hpgi", -c_im, eye_g).reshape(g, p, BRANCH_W)
    c_mat = lanes(jnp.moveaxis(cm_re, 2, 0), jnp.moveaxis(cm_im, 2, 0)).T
    pr, pi = a_pow(jnp.arange(S5_SUB, dtype=F32))
    cb_re = (jnp.einsum("gip,tgp,gpj->tgij", c_re, pr, bbar_re) - jnp.einsum("gip,tgp,gpj->tgij", c_re, pi, bbar_im)
             - jnp.einsum("gip,tgp,gpj->tgij", c_im, pr, bbar_im) - jnp.einsum("gip,tgp,gpj->tgij", c_im, pi, bbar_re))
    taps = jnp.einsum("tgij,gh->tgjhi", cb_re, eye_g).reshape(S5_SUB, BRANCH_W, BRANCH_W)
    rho = jnp.arange(tt) % S5_SUB
    w_in = lanes(*a_pow((S5_SUB - 1 - rho).astype(F32)))
    w_out = lanes(*a_pow((rho + 1).astype(F32)))
    nblk = tt // S5_SUB
    steps = int(math.log2(nblk))
    scan_pw = lanes(*a_pow(S5_SUB * (2.0 ** jnp.arange(steps, dtype=F32))))
    carry_pw = lanes(*a_pow(S5_SUB * jnp.arange(1, nblk + 1, dtype=F32)))
    return (b_mat.astype(BF16), c_mat.astype(BF16), taps.astype(BF16), w_in, w_out, scan_pw, carry_pw)


def _cmul(a_re, a_im, b_re, b_im):
    return a_re * b_re - a_im * b_im, a_re * b_im + a_im * b_re


def _halves(x):
    q = S5_HALF // 2
    return [(x[:, 0:q], x[:, q:2 * q]), (x[:, 2 * q:3 * q], x[:, 3 * q:4 * q])]


def _cscale(x, w):
    out = []
    for (xr, xi), (wr, wi) in zip(_halves(x), _halves(w)):
        out.extend(_cmul(xr, xi, wr, wi))
    return jnp.concatenate(out, axis=1)


def _s5_kernel(p_ref, bm_ref, cm_ref, taps_ref, win_ref, wout_ref, spw_ref, cpw_ref, d_ref,
               wglu_ref, bglu_ref, y_ref, carry_ref):
    tt = p_ref.shape[0]
    nblk = tt // S5_SUB
    hw = BRANCH_W // 2

    @pl.when(pl.program_id(1) == 0)
    def _():
        carry_ref[...] = jnp.zeros_like(carry_ref)

    u = p_ref[:, 0:BRANCH_W]
    z = p_ref[:, BRANCH_W:2 * BRANCH_W]
    ub = u.astype(BF16)

    rho = _iota2((tt, BRANCH_W), 0) % S5_SUB
    y_lo = jnp.zeros((tt, hw), F32)
    y_hi = jnp.zeros((tt, hw), F32)
    for tau in range(S5_SUB):
        us = u if tau == 0 else jnp.where(rho >= tau, pltpu.roll(u, tau, 0), 0.0)
        us = us.astype(BF16)
        y_lo = y_lo + jnp.dot(us[:, 0:hw], taps_ref[tau, 0:hw, 0:hw], preferred_element_type=F32)
        y_hi = y_hi + jnp.dot(us[:, hw:], taps_ref[tau, hw:, hw:], preferred_element_type=F32)

    bu = jnp.concatenate(
        [jnp.dot(ub[:, 0:hw], bm_ref[0:hw, 0:S5_HALF], preferred_element_type=F32),
         jnp.dot(ub[:, hw:], bm_ref[hw:, S5_HALF:], preferred_element_type=F32)], axis=1)
    zw = _cscale(bu, win_ref[...])
    pool = (_iota2((nblk, tt), 1) // S5_SUB == _iota2((nblk, tt), 0)).astype(BF16)
    s = _exact_left_dot(pool, zw)

    row = _iota2((nblk, S5_NSTATE), 0)
    x = s
    for kk in range(int(math.log2(nblk))):
        sh = 1 << kk
        shifted = jnp.where(row >= sh, pltpu.roll(x, sh, 0), 0.0)
        x = x + _cscale(shifted, spw_ref[kk:kk + 1, :])
    carry = carry_ref[...]
    x = x + _cscale(jnp.broadcast_to(carry, (nblk, S5_NSTATE)), cpw_ref[...])
    carry_ref[...] = x[nblk - 1:nblk, :]
    x_prev = jnp.where(row >= 1, pltpu.roll(x, 1, 0), jnp.broadcast_to(carry, (nblk, S5_NSTATE)))

    expand = (_iota2((tt, nblk), 0) // S5_SUB == _iota2((tt, nblk), 1)).astype(BF16)
    xe = _exact_left_dot(expand, x_prev)
    zc = _cscale(xe, wout_ref[...]).astype(BF16)
    y_lo = y_lo + jnp.dot(zc[:, 0:S5_HALF], cm_ref[0:S5_HALF, 0:hw], preferred_element_type=F32)
    y_hi = y_hi + jnp.dot(zc[:, S5_HALF:], cm_ref[S5_HALF:, hw:], preferred_element_type=F32)

    y = jnp.concatenate([y_lo, y_hi], axis=1) + d_ref[...] * u
    y = 0.5 * y * (1.0 + jnp.tanh(math.sqrt(2.0 / math.pi) * (y + 0.044715 * (y * y * y))))
    y = y * _sigmoid(_bdot(y, wglu_ref[...]) + bglu_ref[...])
    y_ref[...] = (y * _silu(z)).astype(y_ref.dtype)


def _s5(pd, tables, d, w_glu, b_glu):
    bsz, t, _ = pd.shape
    tt = min(TIME_TILE, t)
    nblk = tt // S5_SUB
    steps = int(math.log2(nblk))
    b_mat, c_mat, taps, w_in, w_out, scan_pw, carry_pw = tables
    ins = (pd, b_mat, c_mat, taps, w_in, w_out, scan_pw, carry_pw, d.reshape(1, BRANCH_W),
           w_glu.astype(BF16), b_glu.reshape(1, BRANCH_W))
    in_specs = [_tok_spec(tt, SEG_D), _const_spec((BRANCH_W, S5_NSTATE)), _const_spec((S5_NSTATE, BRANCH_W)),
                _const_spec((S5_SUB, BRANCH_W, BRANCH_W)), _const_spec((tt, S5_NSTATE)),
                _const_spec((tt, S5_NSTATE)), _const_spec((steps, S5_NSTATE)),
                _const_spec((nblk, S5_NSTATE)), _const_spec((1, BRANCH_W)),
                _const_spec((BRANCH_W, BRANCH_W)), _const_spec((1, BRANCH_W))]
    scratch = [pltpu.VMEM((1, S5_NSTATE), F32)]
    return _mixer_call(_s5_kernel, ins, in_specs, scratch, bsz, t, "s5")


def _merge_kernel(x_ref, ya_ref, yb_ref, yc_ref, yd_ref, g_ref, wb_ref, wo_ref, fn_ref, o_ref, *,
                  final_norm):
    merged = jnp.zeros((x_ref.shape[0], D_MODEL), F32)
    for n, y_ref in enumerate((ya_ref, yb_ref, yc_ref, yd_ref)):
        proj = jnp.dot(y_ref[...], wb_ref[n], preferred_element_type=F32)
        merged = merged + _sigmoid(g_ref[:, n * D_MODEL:(n + 1) * D_MODEL]) * proj
    out = x_ref[...] + _bdot(merged, wo_ref[...])
    if final_norm:
        out = out * lax.rsqrt(jnp.mean(out * out, axis=-1, keepdims=True) + EPS) * fn_ref[...]
    o_ref[...] = out


def _merge(x2d, ys, gates, w_branch, w_out, fnorm, final_norm):
    n = x2d.shape[0]
    tm = min(ROW_TILE, n)
    row = lambda w: pl.BlockSpec((tm, w), lambda i: (i, 0))
    return pl.pallas_call(
        functools.partial(_merge_kernel, final_norm=final_norm),
        grid=(n // tm,),
        in_specs=[row(D_MODEL), row(BRANCH_W), row(BRANCH_W), row(BRANCH_W), row(BRANCH_W),
                  row(SEG_G),
                  pl.BlockSpec((N_BRANCH, BRANCH_W, D_MODEL), lambda i: (0, 0, 0)),
                  pl.BlockSpec((D_MODEL, D_MODEL), lambda i: (0, 0)),
                  pl.BlockSpec((1, D_MODEL), lambda i: (0, 0))],
        out_specs=row(D_MODEL),
        out_shape=jax.ShapeDtypeStruct((n, D_MODEL), F32),
        compiler_params=pltpu.CompilerParams(dimension_semantics=("arbitrary",),
                                             vmem_limit_bytes=VMEM_LIMIT_BYTES),
        name="merge",
    )(x2d, *ys, gates, w_branch.astype(BF16), w_out.astype(BF16), fnorm.reshape(1, D_MODEL))


def kernel(x, norm_g, w_in, w_branch, w_out, gla_w_a2, gla_b_a, gla_norm, gdn_conv, gdn_a_log,
           gdn_dt_bias, gdn_norm, ret_norm, s5_lambda_re, s5_lambda_im, s5_log_dt, s5_b_re, s5_b_im,
           s5_c_re, s5_c_im, s5_d, s5_w_glu, s5_b_glu, final_norm):
    bsz, t, _ = x.shape
    depth = w_in.shape[0]
    n = bsz * t
    tt = min(TIME_TILE, t)
    ret_tables = _ret_tables(t)
    x2d = x.reshape(n, D_MODEL)
    for l in range(depth):
        pa, pb, pc, pd, pg, ps = _inproj(x2d, norm_g[l].reshape(1, D_MODEL), _pack_w_in(w_in[l]))
        tok = lambda a: a.reshape(bsz, t, a.shape[-1])
        ps3 = tok(ps)
        ya = _gla(tok(pa), ps3, gla_w_a2[l], gla_b_a[l], gla_norm[l])
        yb = _gdn(tok(pb), ps3, gdn_conv[l], gdn_a_log[l], gdn_dt_bias[l], gdn_norm[l])
        yc = _ret(tok(pc), ret_tables, ret_norm[l])
        s5_tables = _s5_tables(s5_lambda_re[l], s5_lambda_im[l], s5_log_dt[l], s5_b_re[l], s5_b_im[l],
                               s5_c_re[l], s5_c_im[l], tt)
        yd = _s5(tok(pd), s5_tables, s5_d[l], s5_w_glu[l], s5_b_glu[l])
        ys = tuple(y.reshape(n, BRANCH_W) for y in (ya, yb, yc, yd))
        x2d = _merge(x2d, ys, pg, w_branch[l], w_out[l], final_norm, final_norm=(l == depth - 1))
    return x2d.reshape(bsz, t, D_MODEL)
```

```python
import functools
import math

import numpy as np
import jax
import jax.numpy as jnp
from jax import lax
from jax.experimental import pallas as pl
from jax.experimental.pallas import tpu as pltpu

F32 = jnp.float32
BF16 = jnp.bfloat16

D_MODEL = 1024
N_BRANCH = 4
BRANCH_W = 512
N_HEADS = 4
CHUNK = 64
EPS = 1e-6

GLA_DK = 64
GLA_DV = 128
GLA_RANK = 16
GLA_TAU = 16.0

GDN_DK = 128
GDN_DV = 128
CONV_K = 4
GDN_QKV = 2 * N_HEADS * GDN_DK + N_HEADS * GDN_DV

RET_DK = 64
RET_DV = 128
ROPE_BASE = 10000.0

S5_GROUP = 16
S5_GROUPS = 32
S5_STATE = 64
S5_SUB = 8
S5_NSTATE = 2 * S5_GROUPS * S5_STATE
S5_HALF = S5_NSTATE // 2
S5_HALF_W = BRANCH_W // 2

LANES = 128
SMALL_W = LANES
GDN_A_OFF = GLA_RANK
GDN_B_OFF = GLA_RANK + N_HEADS

SEG_A = 2 * N_HEADS * GLA_DK + 2 * BRANCH_W
SEG_B = GDN_QKV + BRANCH_W
SEG_C = 2 * N_HEADS * RET_DK + 2 * BRANCH_W
SEG_D = 2 * BRANCH_W
SEG_G = N_BRANCH * D_MODEL
SEG_WIDTHS = (SEG_A, SEG_B, SEG_C, SEG_D, SEG_G, SMALL_W)
SEG_DTYPES = (BF16, BF16, BF16, BF16, BF16, F32)
D_PACKED = sum(SEG_WIDTHS)

VMEM_LIMIT_BYTES = 56 * 1024 * 1024

ROW_TILE = 512
ATTN_TIME_TILE = 256
GDN_TIME_TILE = 128
S5_TIME_TILE = 256


def _sigmoid(x):
    return 1.0 / (1.0 + jnp.exp(-x))


def _silu(x):
    return x * _sigmoid(x)


def _softplus(x):
    return jnp.maximum(x, 0.0) + jnp.log1p(jnp.exp(-jnp.abs(x)))


def _log_sigmoid(x):
    return -_softplus(-x)


def _bdot(a, b):
    return jnp.dot(a.astype(BF16), b.astype(BF16), preferred_element_type=F32)


def _bmm(a, b):
    return jnp.einsum("nck,nkj->ncj", a.astype(BF16), b.astype(BF16), preferred_element_type=F32)


def _bmm_nt(a, b):
    return jnp.einsum("nck,nsk->ncs", a.astype(BF16), b.astype(BF16), preferred_element_type=F32)


def _split2(x):
    hi = x.astype(BF16)
    return hi, (x - hi.astype(F32)).astype(BF16)


def _iota2(shape, dim):
    return lax.broadcasted_iota(jnp.int32, shape, dim)


def _tri_incl(n):
    return (_iota2((n, n), 0) >= _iota2((n, n), 1)).astype(BF16)


def _cumsum_rows(tri_b, x):
    hi, lo = _split2(x)
    return (jnp.einsum("nct,ntl->ncl", tri_b, hi, preferred_element_type=F32)
            + jnp.einsum("nct,ntl->ncl", tri_b, lo, preferred_element_type=F32))


def _block_diag_rows(x, blk_cols):
    _, _, cols = x.shape
    rows = N_HEADS * CHUNK
    mask = (lax.broadcasted_iota(jnp.int32, (1, rows, cols), 1) // CHUNK
            == lax.broadcasted_iota(jnp.int32, (1, rows, cols), 2) // blk_cols)
    return jnp.where(mask, jnp.concatenate([x] * N_HEADS, axis=1), jnp.zeros((), x.dtype))


def _head_norm(o, kind):
    if kind == "group":
        o = o - jnp.mean(o, axis=-1, keepdims=True)
    return o * lax.rsqrt(jnp.mean(o * o, axis=-1, keepdims=True) + EPS)


def _all_batch_call(kernel, ins, in_specs, scratch, bsz, t, tt, name):
    return pl.pallas_call(
        kernel,
        grid=(t // tt,),
        in_specs=in_specs,
        out_specs=pl.BlockSpec((bsz, tt, BRANCH_W), lambda i: (0, i, 0)),
        out_shape=jax.ShapeDtypeStruct((bsz, t, BRANCH_W), BF16),
        scratch_shapes=scratch,
        compiler_params=pltpu.CompilerParams(dimension_semantics=("arbitrary",),
                                             vmem_limit_bytes=VMEM_LIMIT_BYTES),
        name=name,
    )(*ins)


def _tok_all(bsz, tt, width):
    return pl.BlockSpec((bsz, tt, width), lambda i: (0, i, 0))


def _const1(shape):
    return pl.BlockSpec(shape, lambda i: (0,) * len(shape))


CONV_PAD = 8
SEG_B_INDEX = 1
CONV_ROWS = 128


def _inproj_kernel(x_ref, g_ref, w_ref, cw_ref, *refs, tiles_per_seq):
    out_refs, xbuf_ref = refs[:-1], refs[-1]
    tm = x_ref.shape[0]

    @pl.when(pl.program_id(0) % tiles_per_seq == 0)
    def _():
        xbuf_ref[...] = jnp.zeros_like(xbuf_ref)

    x = x_ref[...]
    h = x * lax.rsqrt(jnp.mean(x * x, axis=-1, keepdims=True) + EPS) * g_ref[...]
    h = h.astype(BF16)
    offs = np.concatenate([[0], np.cumsum(SEG_WIDTHS)])

    pb_ref, off_b = out_refs[SEG_B_INDEX], int(offs[SEG_B_INDEX])
    plain = [(pb_ref, off_b, GDN_QKV, SEG_B - GDN_QKV)]
    for idx, ref in enumerate(out_refs):
        if idx != SEG_B_INDEX:
            plain += [(ref, int(offs[idx]), c0, min(512, ref.shape[1] - c0)) for c0 in range(0, ref.shape[1], 512)]
    plain = iter(plain)

    def project_next():
        task = next(plain, None)
        if task is not None:
            ref, off, c0, cw = task
            ref[:, c0:c0 + cw] = jnp.dot(h, w_ref[:, off + c0:off + c0 + cw],
                                         preferred_element_type=F32).astype(ref.dtype)
        return task

    hw = N_HEADS * GDN_DK
    for part, c0 in enumerate(range(0, GDN_QKV, hw)):
        res_all = jnp.dot(h, w_ref[:, off_b + c0:off_b + c0 + hw], preferred_element_type=F32)
        hist_all = xbuf_ref[:, c0:c0 + hw]
        xbuf_ref[:, c0:c0 + hw] = res_all[tm - CONV_PAD:, :]
        for h_i in range(N_HEADS):
            cs = slice(c0 + h_i * GDN_DK, c0 + (h_i + 1) * GDN_DK)
            hl = slice(h_i * GDN_DK, (h_i + 1) * GDN_DK)
            for r0 in range(0, tm, CONV_ROWS):
                res = res_all[r0:r0 + CONV_ROWS, hl]
                prev = hist_all[:, hl] if r0 == 0 else res_all[r0 - CONV_PAD:r0, hl]
                full = jnp.concatenate([prev, res], axis=0)
                conv = cw_ref[CONV_K - 1:CONV_K, cs] * res
                for k in range(1, CONV_K):
                    conv = conv + cw_ref[CONV_K - 1 - k:CONV_K - k, cs] * pltpu.roll(full, k, 0)[CONV_PAD:, :]
                a = _silu(conv)
                if part < 2:
                    scale = GDN_DK ** -0.5 if part == 0 else 1.0
                    a = a * (lax.rsqrt(jnp.sum(a * a, axis=-1, keepdims=True) + EPS) * scale)
                pb_ref[r0:r0 + CONV_ROWS, cs] = a.astype(pb_ref.dtype)
            project_next()
    while project_next() is not None:
        pass


def _inproj(x2d, g, w_packed, conv_w, seq_len):
    n = x2d.shape[0]
    tm = min(ROW_TILE, seq_len)
    out_shape = tuple(jax.ShapeDtypeStruct((n, w), dt) for w, dt in zip(SEG_WIDTHS, SEG_DTYPES))
    return pl.pallas_call(
        functools.partial(_inproj_kernel, tiles_per_seq=seq_len // tm),
        grid=(n // tm,),
        in_specs=[
            pl.BlockSpec((tm, D_MODEL), lambda i: (i, 0)),
            pl.BlockSpec((1, D_MODEL), lambda i: (0, 0)),
            pl.BlockSpec((D_MODEL, D_PACKED), lambda i: (0, 0), pipeline_mode=pl.Buffered(1)),
            pl.BlockSpec((CONV_K, GDN_QKV), lambda i: (0, 0)),
        ],
        out_specs=tuple(pl.BlockSpec((tm, w), lambda i: (i, 0)) for w in SEG_WIDTHS),
        out_shape=out_shape,
        scratch_shapes=[pltpu.VMEM((CONV_PAD, GDN_QKV), F32)],
        compiler_params=pltpu.CompilerParams(dimension_semantics=("arbitrary",),
                                             vmem_limit_bytes=VMEM_LIMIT_BYTES),
        name="inproj",
    )(x2d, g, w_packed, conv_w)


def _pack_w_in(w_in_l):
    sizes = (N_HEADS * GLA_DK, N_HEADS * GLA_DK, BRANCH_W, GLA_RANK, BRANCH_W,
             GDN_QKV, N_HEADS, N_HEADS, BRANCH_W,
             N_HEADS * RET_DK, N_HEADS * RET_DK, BRANCH_W, BRANCH_W,
             BRANCH_W, BRANCH_W, N_BRANCH * D_MODEL)
    offs = np.concatenate([[0], np.cumsum(sizes)])
    (qa, ka, va, lra, za, qkvb, ab, bb, zb, qc, kc, vc, zc, ud, zd, gates) = [
        w_in_l[:, int(offs[i]):int(offs[i + 1])] for i in range(len(sizes))]
    pad = jnp.zeros((D_MODEL, SMALL_W - GLA_RANK - 2 * N_HEADS), w_in_l.dtype)
    return jnp.concatenate([qa, ka, va, za, qkvb, zb, qc, kc, vc, zc, ud, zd, gates,
                            lra, ab, bb, pad], axis=1)


def _linattn_chunk(q_e, k_e, k_s, v, dec_row, st_ref, dk):
    hk = N_HEADS * dk
    hv = v.shape[2]
    dv = hv // N_HEADS
    r64 = lax.broadcasted_iota(jnp.int32, (1, CHUNK, N_HEADS * CHUNK), 1)
    c64 = lax.broadcasted_iota(jnp.int32, (1, CHUNK, N_HEADS * CHUNK), 2) % CHUNK
    scores = jnp.where(r64 >= c64, _bmm_nt(q_e, _block_diag_rows(k_e.astype(BF16), dk)), 0.0)
    st = st_ref[...]
    o = _bmm(scores, _block_diag_rows(v.astype(BF16), dv)) + _bmm_nt(q_e, st)
    kv_t = _bmm(jnp.swapaxes(v.astype(F32), 1, 2), k_s)
    smask = (lax.broadcasted_iota(jnp.int32, (1, hv, hk), 1) // dv
             == lax.broadcasted_iota(jnp.int32, (1, hv, hk), 2) // dk)
    st_ref[...] = st * dec_row + jnp.where(smask, kv_t, 0.0)
    return o


def _store_heads(y_ref, rows, o, z, gn_ref, kind, dv):
    for h in range(N_HEADS):
        hs = slice(h * dv, (h + 1) * dv)
        y = _head_norm(o[:, :, hs], kind) * gn_ref[...] * _silu(z[:, :, hs])
        y_ref[:, rows, hs] = y.astype(y_ref.dtype)


def _gla_kernel(p_ref, s_ref, wa2_ref, ba_ref, gn_ref, y_ref, st_ref):
    nb, tt = p_ref.shape[0], p_ref.shape[1]

    @pl.when(pl.program_id(0) == 0)
    def _():
        st_ref[...] = jnp.zeros_like(st_ref)

    hk = N_HEADS * GLA_DK
    tri_b = jnp.broadcast_to(_tri_incl(CHUNK)[None], (nb, CHUNK, CHUNK))
    for c in range(tt // CHUNK):
        rows = slice(c * CHUNK, (c + 1) * CHUNK)
        q = p_ref[:, rows, 0:hk].astype(F32) * (GLA_DK ** -0.5)
        k = p_ref[:, rows, hk:2 * hk].astype(F32)
        v = p_ref[:, rows, 2 * hk:2 * hk + BRANCH_W]
        z = p_ref[:, rows, 2 * hk + BRANCH_W:2 * hk + 2 * BRANCH_W].astype(F32)
        logits = jnp.stack([_bdot(s_ref[b, rows, 0:GLA_RANK], wa2_ref[...]) for b in range(nb)]) + ba_ref[...]
        log_a = _log_sigmoid(logits) * (1.0 / GLA_TAU)
        cum = _cumsum_rows(tri_b, log_a)
        last = cum[:, CHUNK - 1:CHUNK, :]
        o = _linattn_chunk(q * jnp.exp(cum), k * jnp.exp(-cum), k * jnp.exp(last - cum), v,
                           jnp.exp(last), st_ref, GLA_DK)
        _store_heads(y_ref, rows, o, z, gn_ref, "rms", GLA_DV)


def _ret_kernel(p_ref, cos_ref, sin_ref, qe_ref, ke_ref, ks_ref, dec_ref, gn_ref, y_ref, st_ref):
    nb, tt = p_ref.shape[0], p_ref.shape[1]

    @pl.when(pl.program_id(0) == 0)
    def _():
        st_ref[...] = jnp.zeros_like(st_ref)

    hk = N_HEADS * RET_DK
    half = RET_DK // 2
    first_half = (lax.broadcasted_iota(jnp.int32, (1, CHUNK, hk), 2) % RET_DK) < half

    def rotary(x, cos, sin):
        swapped = jnp.where(first_half, pltpu.roll(x, hk - half, 2), pltpu.roll(x, half, 2))
        return x * cos + swapped * sin

    dec = jnp.broadcast_to(dec_ref[...][None], (nb, 1, hk))
    for c in range(tt // CHUNK):
        rows = slice(c * CHUNK, (c + 1) * CHUNK)
        cos = cos_ref[rows, :][None]
        sin = sin_ref[rows, :][None]
        q = rotary(p_ref[:, rows, 0:hk].astype(F32), cos, sin)
        k = rotary(p_ref[:, rows, hk:2 * hk].astype(F32), cos, sin) * (RET_DK ** -0.5)
        v = p_ref[:, rows, 2 * hk:2 * hk + BRANCH_W]
        z = p_ref[:, rows, 2 * hk + BRANCH_W:2 * hk + 2 * BRANCH_W].astype(F32)
        o = _linattn_chunk(q * qe_ref[...][None], k * ke_ref[...][None], k * ks_ref[...][None], v,
                           dec, st_ref, RET_DK)
        _store_heads(y_ref, rows, o, z, gn_ref, "group", RET_DV)


def _gla(pa, ps, w_a2, b_a, g_norm):
    bsz, t, _ = pa.shape
    tt = min(ATTN_TIME_TILE, t)
    hk = N_HEADS * GLA_DK
    ins = (pa, ps, w_a2.astype(BF16), b_a.reshape(1, hk), g_norm.reshape(1, GLA_DV))
    in_specs = [_tok_all(bsz, tt, SEG_A), _tok_all(bsz, tt, SMALL_W), _const1((GLA_RANK, hk)),
                _const1((1, hk)), _const1((1, GLA_DV))]
    scratch = [pltpu.VMEM((bsz, BRANCH_W, hk), F32)]
    return _all_batch_call(_gla_kernel, ins, in_specs, scratch, bsz, t, tt, "gla")


def _ret_tables(t):
    half = RET_DK // 2
    inv_freq = ROPE_BASE ** (-jnp.arange(half, dtype=F32) / half)
    ang = jnp.arange(t, dtype=F32)[:, None] * inv_freq
    cos, sin = jnp.cos(ang), jnp.sin(ang)
    cos_t = jnp.tile(jnp.concatenate([cos, cos], axis=1), (1, N_HEADS))
    sin_t = jnp.tile(jnp.concatenate([-sin, sin], axis=1), (1, N_HEADS))
    log_gamma = jnp.log1p(-jnp.exp2(-5.0 - jnp.arange(N_HEADS, dtype=F32)))
    lg = jnp.repeat(log_gamma, RET_DK)[None, :]
    cum = jnp.arange(1, CHUNK + 1, dtype=F32)[:, None] * lg
    last = cum[-1:, :]
    return cos_t, sin_t, jnp.exp(cum), jnp.exp(-cum), jnp.exp(last - cum), jnp.exp(last)


def _ret(pc, tables, g_norm):
    bsz, t, _ = pc.shape
    tt = min(ATTN_TIME_TILE, t)
    hk = N_HEADS * RET_DK
    cos_t, sin_t, qe, ke, ks, dec = tables
    ins = (pc, cos_t, sin_t, qe, ke, ks, dec, g_norm.reshape(1, RET_DV))
    pos_spec = pl.BlockSpec((tt, hk), lambda i: (i, 0))
    in_specs = [_tok_all(bsz, tt, SEG_C), pos_spec, pos_spec, _const1((CHUNK, hk)),
                _const1((CHUNK, hk)), _const1((CHUNK, hk)), _const1((1, hk)), _const1((1, RET_DV))]
    scratch = [pltpu.VMEM((bsz, BRANCH_W, hk), F32)]
    return _all_batch_call(_ret_kernel, ins, in_specs, scratch, bsz, t, tt, "retention")


GDN_HD = N_HEADS * GDN_DK
GDN_PAIR = 2 * GDN_DK


def _head_lanes(x, lane0, width):
    n, rows = x.shape[0], x.shape[1]
    total = N_HEADS * width
    lane = lax.broadcasted_iota(jnp.int32, (1, 1, total), 2)
    out = jnp.broadcast_to(x[:, :, lane0:lane0 + 1], (n, rows, total))
    for h in range(1, N_HEADS):
        out = jnp.where(lane >= h * width, jnp.broadcast_to(x[:, :, lane0 + h:lane0 + h + 1], (n, rows, total)), out)
    return out


def _gdn_kernel(p_ref, s_ref, gpar_ref, gn_ref, y_ref, st_ref):
    nb, tt = p_ref.shape[0], p_ref.shape[1]
    nc = tt // CHUNK
    n = nc * nb

    @pl.when(pl.program_id(0) == 0)
    def _():
        st_ref[...] = jnp.zeros_like(st_ref)

    def chunks(c_lo, c_hi, ref=p_ref):
        return jnp.concatenate([ref[:, c * CHUNK:(c + 1) * CHUNK, c_lo:c_hi] for c in range(nc)],
                               axis=0).astype(F32)

    q = chunks(0, GDN_HD)
    k = chunks(GDN_HD, 2 * GDN_HD)
    v = chunks(2 * GDN_HD, 3 * GDN_HD)
    z = chunks(GDN_QKV, GDN_QKV + BRANCH_W)
    s = chunks(0, SMALL_W, s_ref)

    g = -jnp.exp(gpar_ref[0:1, :]) * _softplus(s + gpar_ref[1:2, :])
    beta = _head_lanes(_sigmoid(s), GDN_B_OFF, GDN_DK)

    r64 = lax.broadcasted_iota(jnp.int32, (1, CHUNK, N_HEADS * CHUNK), 1)
    c64 = lax.broadcasted_iota(jnp.int32, (1, CHUNK, N_HEADS * CHUNK), 2) % CHUNK
    tri_b = jnp.broadcast_to(_tri_incl(CHUNK)[None], (n, CHUNK, CHUNK))
    dlog = _cumsum_rows(tri_b, jnp.where(r64 > c64, _head_lanes(g, GDN_A_OFF, CHUNK), 0.0))
    decay = jnp.where(r64 >= c64, jnp.exp(dlog), 0.0)
    gcol = _head_lanes(_cumsum_rows(tri_b, g), GDN_A_OFF, GDN_DK)
    glast = gcol[:, CHUNK - 1:CHUNK, :]
    egc = jnp.exp(gcol)
    dec = jnp.exp(glast)

    kb = k * beta
    q_e = q * egc
    k_st = jnp.swapaxes(k * jnp.exp(glast - gcol), 1, 2).astype(BF16)
    kq = _bmm_nt(jnp.concatenate([kb, q], axis=1), _block_diag_rows(k.astype(BF16), GDN_DK))
    low = jnp.where(r64 > c64, kq[:, 0:CHUNK, :] * decay, 0.0)
    attn = kq[:, CHUNK:, :] * decay

    m = -low
    inv = jnp.where(r64 == c64, 1.0, 0.0) + m
    levels = int(math.log2(CHUNK)) - 1
    for lvl in range(levels):
        mbd = _block_diag_rows(m.astype(BF16), CHUNK)
        if lvl == 0:
            m = _bmm(m, mbd)
        else:
            both = _bmm(jnp.concatenate([m, inv], axis=1), mbd)
            m = both[:, 0:CHUNK, :]
            inv = inv + both[:, CHUNK:, :]
    inv = inv + _bmm(inv, _block_diag_rows(m.astype(BF16), CHUNK))

    vb = v * beta
    kbe = kb * egc
    us, ws = [], []
    for h in range(N_HEADS):
        hs = slice(h * GDN_DV, (h + 1) * GDN_DV)
        uw = _bmm(inv[:, :, h * CHUNK:(h + 1) * CHUNK], jnp.concatenate([vb[:, :, hs], kbe[:, :, hs]], axis=-1))
        us.append(uw[:, :, 0:GDN_DV])
        ws.append(uw[:, :, GDN_DV:])
    u = jnp.concatenate(us, axis=-1)
    wq = jnp.concatenate([jnp.concatenate(ws, axis=-1), q_e], axis=1)

    pair_mask = (lax.broadcasted_iota(jnp.int32, (1, GDN_PAIR, GDN_PAIR), 1) // GDN_DK
                 == lax.broadcasted_iota(jnp.int32, (1, GDN_PAIR, GDN_PAIR), 2) // GDN_DV)
    for c in range(nc):
        bs = slice(c * nb, (c + 1) * nb)
        ws_parts = []
        for pr in range(N_HEADS // 2):
            ps_ = slice(pr * GDN_PAIR, (pr + 1) * GDN_PAIR)
            ws_parts.append(_bmm(wq[bs, :, ps_], st_ref[:, pr]))
        w_s = jnp.concatenate(ws_parts, axis=-1)
        v_new = u[bs] - w_s[:, 0:CHUNK, :]
        o = w_s[:, CHUNK:, :] + _bmm(attn[bs], _block_diag_rows(v_new.astype(BF16), GDN_DV))
        for pr in range(N_HEADS // 2):
            ps_ = slice(pr * GDN_PAIR, (pr + 1) * GDN_PAIR)
            kv = _bmm(k_st[bs, ps_, :], v_new[:, :, ps_])
            st_ref[:, pr] = st_ref[:, pr] * dec[bs, :, ps_] + jnp.where(pair_mask, kv, 0.0)
        _store_heads(y_ref, slice(c * CHUNK, (c + 1) * CHUNK), o, z[bs], gn_ref, "rms", GDN_DV)


def _gdn(pb, ps, a_log, dt_bias, g_norm):
    bsz, t, _ = pb.shape
    tt = min(GDN_TIME_TILE, t)
    lane_pad = lambda x: jnp.pad(x.reshape(1, N_HEADS), ((0, 0), (GDN_A_OFF, SMALL_W - GDN_A_OFF - N_HEADS)))
    gpar = jnp.concatenate([lane_pad(a_log), lane_pad(dt_bias)], axis=0)
    ins = (pb, ps, gpar, g_norm.reshape(1, GDN_DV))
    in_specs = [_tok_all(bsz, tt, SEG_B), _tok_all(bsz, tt, SMALL_W), _const1((2, SMALL_W)), _const1((1, GDN_DV))]
    scratch = [pltpu.VMEM((bsz, N_HEADS // 2, GDN_PAIR, GDN_PAIR), F32)]
    return _all_batch_call(_gdn_kernel, ins, in_specs, scratch, bsz, t, tt, "gdn")


def _expand_lanes(x, reps):
    w = x.shape[-1]
    sel = jnp.tile(jnp.eye(w, dtype=F32), (1, reps))
    return jnp.dot(x, sel, precision=lax.Precision.HIGHEST)


def _s5_tables(lam_re, lam_im, log_dt, b_re, b_im, c_re, c_im, tt):
    g, p, gi = S5_GROUPS, S5_STATE, S5_GROUP
    hg = g // 2
    dt = jnp.exp(log_dt)[:, None]

    def a_pow(n):
        n = jnp.asarray(n, F32)[..., None, None]
        mag = jnp.exp(lam_re * dt * n)
        return mag * jnp.cos(lam_im * dt * n), mag * jnp.sin(lam_im * dt * n)

    abar_re, abar_im = a_pow(1.0)
    nr, ni = abar_re - 1.0, abar_im
    den = jnp.square(lam_re) + jnp.square(lam_im)
    coef_re = (nr * lam_re + ni * lam_im) / den
    coef_im = (ni * lam_re - nr * lam_im) / den
    bbar_re = coef_re[..., None] * b_re - coef_im[..., None] * b_im
    bbar_im = coef_re[..., None] * b_im + coef_im[..., None] * b_re

    def lanes(re, im):
        lead = re.shape[:-2]
        parts = [re[..., :hg, :].reshape(lead + (hg * p,)), im[..., :hg, :].reshape(lead + (hg * p,)),
                 re[..., hg:, :].reshape(lead + (hg * p,)), im[..., hg:, :].reshape(lead + (hg * p,))]
        return jnp.concatenate(parts, axis=-1)

    grp_row = jnp.arange(BRANCH_W)[:, None] // gi % hg
    b_mask = grp_row == (jnp.arange(hg * p)[None, :] // p)
    bt = lambda b: _expand_lanes(jnp.swapaxes(b, 1, 2).reshape(BRANCH_W, p), hg) * b_mask
    b_mat = jnp.concatenate([bt(bbar_re), bt(bbar_im)], axis=1).reshape(2, S5_HALF_W, S5_HALF)
    st_row = jnp.arange(hg * p)[:, None] // p
    c_mask = st_row == (jnp.arange(S5_HALF_W)[None, :] // gi)
    ct = lambda c: (_expand_lanes(jnp.swapaxes(c, 1, 2).reshape(2 * hg * p, gi), hg).reshape(2, hg * p, S5_HALF_W)
                    * c_mask)
    c_mat = jnp.concatenate([ct(c_re), ct(-c_im)], axis=1)
    pr, pi = a_pow(jnp.arange(S5_SUB, dtype=F32))
    cb = (jnp.einsum("gip,tgp,gpj->tgji", c_re, pr, bbar_re) - jnp.einsum("gip,tgp,gpj->tgji", c_re, pi, bbar_im)
          - jnp.einsum("gip,tgp,gpj->tgji", c_im, pr, bbar_im) - jnp.einsum("gip,tgp,gpj->tgji", c_im, pi, bbar_re))
    t_mask = (jnp.arange(S5_HALF_W)[:, None] // gi) == (jnp.arange(S5_HALF_W)[None, :] // gi)
    taps = (_expand_lanes(cb.reshape(S5_SUB * BRANCH_W, gi), hg).reshape(S5_SUB, 2, S5_HALF_W, S5_HALF_W)
            * t_mask)
    rho = jnp.arange(S5_SUB, dtype=F32)
    w_in = lanes(*a_pow(S5_SUB - 1 - rho))
    w_out = lanes(*a_pow(rho + 1))
    nblk = tt // S5_SUB
    steps = int(math.log2(nblk))
    scan_pw = lanes(*a_pow(S5_SUB * (2.0 ** jnp.arange(steps, dtype=F32))))
    carry_pw = lanes(*a_pow(S5_SUB * jnp.arange(1, nblk + 1, dtype=F32)))
    return (b_mat.astype(BF16), c_mat.astype(BF16), taps.astype(BF16), w_in, w_out, scan_pw, carry_pw)


def _cmul(a_re, a_im, b_re, b_im):
    return a_re * b_re - a_im * b_im, a_re * b_im + a_im * b_re


def _halves(x):
    q = S5_HALF // 2
    return [(x[:, 0:q], x[:, q:2 * q]), (x[:, 2 * q:3 * q], x[:, 3 * q:4 * q])]


def _cscale(x, w):
    out = []
    for (xr, xi), (wr, wi) in zip(_halves(x), _halves(w)):
        out.extend(_cmul(xr, xi, wr, wi))
    return jnp.concatenate(out, axis=1)


def _s5_kernel(p_ref, bm_ref, cm_ref, taps_ref, win_ref, wout_ref, spw_ref, cpw_ref, d_ref,
               wglu_ref, bglu_ref, y_ref, carry_ref, acc_ref):
    tt = p_ref.shape[0]
    nblk = tt // S5_SUB
    hw = S5_HALF_W
    dot = functools.partial(jnp.dot, preferred_element_type=F32)

    @pl.when(pl.program_id(1) == 0)
    def _():
        carry_ref[...] = jnp.zeros_like(carry_ref)

    u = p_ref[:, 0:BRANCH_W]
    z = p_ref[:, BRANCH_W:2 * BRANCH_W].astype(F32)
    r_i = _iota2((tt, tt), 0)
    t_i = _iota2((tt, tt), 1)
    perm = (t_i == S5_SUB * (r_i % nblk) + r_i // nblk).astype(BF16)
    unperm = (r_i == S5_SUB * (t_i % nblk) + t_i // nblk).astype(BF16)
    up = dot(perm, u.astype(BF16)).astype(BF16)

    for half in range(2):
        hs = slice(half * hw, (half + 1) * hw)
        acc_ref[:, hs] = dot(up[:, hs], taps_ref[0, half])
        for tau in range(1, S5_SUB):
            acc_ref[tau * nblk:, hs] += dot(up[0:tt - tau * nblk, hs], taps_ref[tau, half])

    bu = jnp.concatenate([dot(up[:, 0:hw], bm_ref[0]), dot(up[:, hw:], bm_ref[1])], axis=1)
    s = _cscale(bu[0:nblk], win_ref[0:1, :])
    for rho in range(1, S5_SUB):
        s = s + _cscale(bu[rho * nblk:(rho + 1) * nblk], win_ref[rho:rho + 1, :])

    row = _iota2((nblk, S5_NSTATE), 0)
    x = s
    for kk in range(int(math.log2(nblk))):
        sh = 1 << kk
        shifted = jnp.where(row >= sh, pltpu.roll(x, sh, 0), 0.0)
        x = x + _cscale(shifted, spw_ref[kk:kk + 1, :])
    carry = jnp.broadcast_to(carry_ref[...], (nblk, S5_NSTATE))
    x = x + _cscale(carry, cpw_ref[...])
    carry_ref[...] = x[nblk - 1:nblk, :]
    x_prev = jnp.where(row >= 1, pltpu.roll(x, 1, 0), carry)

    zc = jnp.concatenate([_cscale(x_prev, wout_ref[rho:rho + 1, :]).astype(BF16) for rho in range(S5_SUB)], axis=0)
    acc_ref[:, 0:hw] += dot(zc[:, 0:S5_HALF], cm_ref[0])
    acc_ref[:, hw:] += dot(zc[:, S5_HALF:], cm_ref[1])

    hi, lo = _split2(acc_ref[...])
    uf = u.astype(F32)
    y = dot(unperm, hi) + dot(unperm, lo) + d_ref[...] * uf
    y = 0.5 * y * (1.0 + jnp.tanh(math.sqrt(2.0 / math.pi) * (y + 0.044715 * (y * y * y))))
    y = y * _sigmoid(_bdot(y, wglu_ref[...]) + bglu_ref[...])
    y_ref[...] = (y * _silu(z)).astype(y_ref.dtype)


def _s5(pd, tables, d, w_glu, b_glu):
    bsz, t, _ = pd.shape
    tt = min(S5_TIME_TILE, t)
    nblk = tt // S5_SUB
    steps = int(math.log2(nblk))
    b_mat, c_mat, taps, w_in, w_out, scan_pw, carry_pw = tables
    ins = (pd, b_mat, c_mat, taps, w_in, w_out, scan_pw, carry_pw, d.reshape(1, BRANCH_W),
           w_glu.astype(BF16), b_glu.reshape(1, BRANCH_W))
    const = lambda shape: pl.BlockSpec(shape, lambda b, i: (0,) * len(shape))
    in_specs = [pl.BlockSpec((None, tt, SEG_D), lambda b, i: (b, i, 0)),
                const((2, S5_HALF_W, S5_HALF)), const((2, S5_HALF, S5_HALF_W)),
                const((S5_SUB, 2, S5_HALF_W, S5_HALF_W)), const((S5_SUB, S5_NSTATE)),
                const((S5_SUB, S5_NSTATE)), const((steps, S5_NSTATE)), const((nblk, S5_NSTATE)),
                const((1, BRANCH_W)), const((BRANCH_W, BRANCH_W)), const((1, BRANCH_W))]
    return pl.pallas_call(
        _s5_kernel,
        grid=(bsz, t // tt),
        in_specs=in_specs,
        out_specs=pl.BlockSpec((None, tt, BRANCH_W), lambda b, i: (b, i, 0)),
        out_shape=jax.ShapeDtypeStruct((bsz, t, BRANCH_W), BF16),
        scratch_shapes=[pltpu.VMEM((1, S5_NSTATE), F32), pltpu.VMEM((tt, BRANCH_W), F32)],
        compiler_params=pltpu.CompilerParams(dimension_semantics=("arbitrary", "arbitrary"),
                                             vmem_limit_bytes=VMEM_LIMIT_BYTES),
        name="s5",
    )(*ins)


def _merge_kernel(x_ref, ya_ref, yb_ref, yc_ref, yd_ref, g_ref, wb_ref, wo_ref, fn_ref, o_ref, *,
                  final_norm):
    merged = jnp.zeros((x_ref.shape[0], D_MODEL), F32)
    for n, y_ref in enumerate((ya_ref, yb_ref, yc_ref, yd_ref)):
        proj = jnp.dot(y_ref[...], wb_ref[n], preferred_element_type=F32)
        merged = merged + _sigmoid(g_ref[:, n * D_MODEL:(n + 1) * D_MODEL].astype(F32)) * proj
    out = x_ref[...] + _bdot(merged, wo_ref[...])
    if final_norm:
        out = out * lax.rsqrt(jnp.mean(out * out, axis=-1, keepdims=True) + EPS) * fn_ref[...]
    o_ref[...] = out


def _merge(x2d, ys, gates, w_branch, w_out, fnorm, final_norm):
    n = x2d.shape[0]
    tm = min(ROW_TILE, n)
    row = lambda w: pl.BlockSpec((tm, w), lambda i: (i, 0))
    return pl.pallas_call(
        functools.partial(_merge_kernel, final_norm=final_norm),
        grid=(n // tm,),
        in_specs=[row(D_MODEL), row(BRANCH_W), row(BRANCH_W), row(BRANCH_W), row(BRANCH_W),
                  row(SEG_G),
                  pl.BlockSpec((N_BRANCH, BRANCH_W, D_MODEL), lambda i: (0, 0, 0)),
                  pl.BlockSpec((D_MODEL, D_MODEL), lambda i: (0, 0)),
                  pl.BlockSpec((1, D_MODEL), lambda i: (0, 0))],
        out_specs=row(D_MODEL),
        out_shape=jax.ShapeDtypeStruct((n, D_MODEL), F32),
        compiler_params=pltpu.CompilerParams(dimension_semantics=("arbitrary",),
                                             vmem_limit_bytes=VMEM_LIMIT_BYTES),
        name="merge",
    )(x2d, *ys, gates, w_branch, w_out, fnorm.reshape(1, D_MODEL))


def kernel(x, norm_g, w_in, w_branch, w_out, gla_w_a2, gla_b_a, gla_norm, gdn_conv, gdn_a_log,
           gdn_dt_bias, gdn_norm, ret_norm, s5_lambda_re, s5_lambda_im, s5_log_dt, s5_b_re, s5_b_im,
           s5_c_re, s5_c_im, s5_d, s5_w_glu, s5_b_glu, final_norm):
    bsz, t, _ = x.shape
    depth = w_in.shape[0]
    n = bsz * t
    ret_tables = _ret_tables(t)
    w_in_b, w_branch_b, w_out_b = w_in.astype(BF16), w_branch.astype(BF16), w_out.astype(BF16)
    x2d = x.reshape(n, D_MODEL)
    for l in range(depth):
        pa, pb, pc, pd, pg, ps = _inproj(x2d, norm_g[l].reshape(1, D_MODEL), _pack_w_in(w_in_b[l]),
                                         gdn_conv[l], t)
        tok = lambda a: a.reshape(bsz, t, a.shape[-1])
        ps3 = tok(ps)
        ya = _gla(tok(pa), ps3, gla_w_a2[l], gla_b_a[l], gla_norm[l])
        yb = _gdn(tok(pb), ps3, gdn_a_log[l], gdn_dt_bias[l], gdn_norm[l])
        yc = _ret(tok(pc), ret_tables, ret_norm[l])
        s5_tables = _s5_tables(s5_lambda_re[l], s5_lambda_im[l], s5_log_dt[l], s5_b_re[l], s5_b_im[l],
                               s5_c_re[l], s5_c_im[l], min(S5_TIME_TILE, t))
        yd = _s5(tok(pd), s5_tables, s5_d[l], s5_w_glu[l], s5_b_glu[l])
        ys = tuple(y.reshape(n, BRANCH_W) for y in (ya, yb, yc, yd))
        x2d = _merge(x2d, ys, pg, w_branch_b[l], w_out_b[l], final_norm, final_norm=(l == depth - 1))
    return x2d.reshape(bsz, t, D_MODEL)
```

```python
import functools
import math

import numpy as np
import jax
import jax.numpy as jnp
from jax import lax
from jax.experimental import pallas as pl
from jax.experimental.pallas import tpu as pltpu

F32 = jnp.float32
BF16 = jnp.bfloat16

D_MODEL = 1024
N_BRANCH = 4
BRANCH_W = 512
N_HEADS = 4
CHUNK = 64
EPS = 1e-6

GLA_DK = 64
GLA_DV = 128
GLA_RANK = 16
GLA_TAU = 16.0

GDN_DK = 128
GDN_DV = 128
CONV_K = 4
GDN_QKV = 2 * N_HEADS * GDN_DK + N_HEADS * GDN_DV

RET_DK = 64
RET_DV = 128
ROPE_BASE = 10000.0

S5_GROUP = 16
S5_GROUPS = 32
S5_STATE = 64
S5_SUB = 8
S5_NSTATE = 2 * S5_GROUPS * S5_STATE
S5_HALF = S5_NSTATE // 2
S5_HALF_W = BRANCH_W // 2

LANES = 128
SMALL_W = LANES
GDN_A_OFF = GLA_RANK
GDN_B_OFF = GLA_RANK + N_HEADS

SEG_A = 2 * N_HEADS * GLA_DK + 2 * BRANCH_W
SEG_B = GDN_QKV + BRANCH_W
SEG_C = 2 * N_HEADS * RET_DK + 2 * BRANCH_W
SEG_D = 2 * BRANCH_W
SEG_G = N_BRANCH * D_MODEL
SEG_WIDTHS = (SEG_A, SEG_B, SEG_C, SEG_D, SEG_G, SMALL_W)
SEG_DTYPES = (BF16, BF16, BF16, BF16, BF16, F32)
D_PACKED = sum(SEG_WIDTHS)

VMEM_LIMIT_BYTES = 56 * 1024 * 1024

ROW_TILE = 512
ATTN_TIME_TILE = 256
GDN_TIME_TILE = 128
S5_TIME_TILE = 256


def _sigmoid(x):
    return 1.0 / (1.0 + jnp.exp(-x))


def _silu(x):
    return x * _sigmoid(x)


def _softplus(x):
    return jnp.maximum(x, 0.0) + jnp.log(1.0 + jnp.exp(-jnp.abs(x)))


def _log_sigmoid(x):
    return -_softplus(-x)


def _bdot(a, b):
    return jnp.dot(a.astype(BF16), b.astype(BF16), preferred_element_type=F32)


def _bmm(a, b):
    return jnp.einsum("nck,nkj->ncj", a.astype(BF16), b.astype(BF16), preferred_element_type=F32)


def _bmm_nt(a, b):
    return jnp.einsum("nck,nsk->ncs", a.astype(BF16), b.astype(BF16), preferred_element_type=F32)


def _split2(x):
    hi = x.astype(BF16)
    return hi, (x - hi.astype(F32)).astype(BF16)


def _iota2(shape, dim):
    return lax.broadcasted_iota(jnp.int32, shape, dim)


def _tri_incl(n):
    return (_iota2((n, n), 0) >= _iota2((n, n), 1)).astype(BF16)


def _cumsum_rows(tri_b, x):
    hi, lo = _split2(x)
    return (jnp.einsum("nct,ntl->ncl", tri_b, hi, preferred_element_type=F32)
            + jnp.einsum("nct,ntl->ncl", tri_b, lo, preferred_element_type=F32))


def _block_diag_rows(x, blk_cols):
    _, _, cols = x.shape
    rows = N_HEADS * CHUNK
    mask = (lax.broadcasted_iota(jnp.int32, (1, rows, cols), 1) // CHUNK
            == lax.broadcasted_iota(jnp.int32, (1, rows, cols), 2) // blk_cols)
    return jnp.where(mask, jnp.concatenate([x] * N_HEADS, axis=1), jnp.zeros((), x.dtype))


def _head_norm(o, kind):
    if kind == "group":
        o = o - jnp.mean(o, axis=-1, keepdims=True)
    return o * lax.rsqrt(jnp.mean(o * o, axis=-1, keepdims=True) + EPS)


def _all_batch_call(kernel, ins, in_specs, scratch, bsz, t, tt, name):
    return pl.pallas_call(
        kernel,
        grid=(t // tt,),
        in_specs=in_specs,
        out_specs=pl.BlockSpec((bsz, tt, BRANCH_W), lambda i: (0, i, 0)),
        out_shape=jax.ShapeDtypeStruct((bsz, t, BRANCH_W), BF16),
        scratch_shapes=scratch,
        compiler_params=pltpu.CompilerParams(dimension_semantics=("arbitrary",),
                                             vmem_limit_bytes=VMEM_LIMIT_BYTES),
        name=name,
    )(*ins)


def _tok_all(bsz, tt, width):
    return pl.BlockSpec((bsz, tt, width), lambda i: (0, i, 0))


def _const1(shape):
    return pl.BlockSpec(shape, lambda i: (0,) * len(shape))


CONV_PAD = 8
SEG_B_INDEX = 1
SEG_D_INDEX = 3
CONV_ROWS = 128


def _inproj_kernel(x_ref, g_ref, w_ref, cw_ref, *refs, tiles_per_seq):
    s5_tabs, refs = refs[:N_S5_TABLES], refs[N_S5_TABLES:]
    out_refs, (xbuf_ref, carry_ref, acc_ref) = refs[:-3], refs[-3:]
    tm = x_ref.shape[0]

    @pl.when(pl.program_id(0) % tiles_per_seq == 0)
    def _():
        xbuf_ref[...] = jnp.zeros_like(xbuf_ref)
        carry_ref[...] = jnp.zeros_like(carry_ref)

    x = x_ref[...]
    h = x * lax.rsqrt(jnp.mean(x * x, axis=-1, keepdims=True) + EPS) * g_ref[...]
    h = h.astype(BF16)
    offs = np.concatenate([[0], np.cumsum(SEG_WIDTHS)])

    pb_ref, off_b = out_refs[SEG_B_INDEX], int(offs[SEG_B_INDEX])
    yd_ref, off_d = out_refs[SEG_D_INDEX], int(offs[SEG_D_INDEX])
    plain = [(pb_ref, off_b, GDN_QKV, SEG_B - GDN_QKV)]
    for idx, ref in enumerate(out_refs):
        if idx not in (SEG_B_INDEX, SEG_D_INDEX):
            plain += [(ref, int(offs[idx]), c0, min(512, ref.shape[1] - c0)) for c0 in range(0, ref.shape[1], 512)]
    plain = iter(plain)

    def project_next():
        task = next(plain, None)
        if task is not None:
            ref, off, c0, cw = task
            ref[:, c0:c0 + cw] = jnp.dot(h, w_ref[:, off + c0:off + c0 + cw],
                                         preferred_element_type=F32).astype(ref.dtype)
        return task

    u = jnp.dot(h, w_ref[:, off_d:off_d + BRANCH_W], preferred_element_type=F32).astype(BF16)
    z = jnp.dot(h, w_ref[:, off_d + BRANCH_W:off_d + 2 * BRANCH_W], preferred_element_type=F32)
    z = z.astype(BF16).astype(F32)
    yd_ref[...] = _s5_tile(u, z, s5_tabs, carry_ref, acc_ref, between=project_next).astype(yd_ref.dtype)

    hw = N_HEADS * GDN_DK
    for part, c0 in enumerate(range(0, GDN_QKV, hw)):
        res_all = jnp.dot(h, w_ref[:, off_b + c0:off_b + c0 + hw], preferred_element_type=F32)
        hist_all = xbuf_ref[:, c0:c0 + hw]
        xbuf_ref[:, c0:c0 + hw] = res_all[tm - CONV_PAD:, :]
        for h_i in range(N_HEADS):
            cs = slice(c0 + h_i * GDN_DK, c0 + (h_i + 1) * GDN_DK)
            hl = slice(h_i * GDN_DK, (h_i + 1) * GDN_DK)
            for r0 in range(0, tm, CONV_ROWS):
                res = res_all[r0:r0 + CONV_ROWS, hl]
                prev = hist_all[:, hl] if r0 == 0 else res_all[r0 - CONV_PAD:r0, hl]
                full = jnp.concatenate([prev, res], axis=0)
                conv = cw_ref[CONV_K - 1:CONV_K, cs] * res
                for k in range(1, CONV_K):
                    conv = conv + cw_ref[CONV_K - 1 - k:CONV_K - k, cs] * pltpu.roll(full, k, 0)[CONV_PAD:, :]
                a = _silu(conv)
                if part < 2:
                    scale = GDN_DK ** -0.5 if part == 0 else 1.0
                    a = a * (lax.rsqrt(jnp.sum(a * a, axis=-1, keepdims=True) + EPS) * scale)
                pb_ref[r0:r0 + CONV_ROWS, cs] = a.astype(pb_ref.dtype)
            project_next()
    while project_next() is not None:
        pass


def _inproj(x2d, g, w_packed, conv_w, s5_ins, seq_len):
    n = x2d.shape[0]
    tm = min(S5_TIME_TILE, seq_len)
    widths = tuple(BRANCH_W if i == SEG_D_INDEX else w for i, w in enumerate(SEG_WIDTHS))
    out_shape = tuple(jax.ShapeDtypeStruct((n, w), dt) for w, dt in zip(widths, SEG_DTYPES))
    const = lambda a: pl.BlockSpec(a.shape, lambda i: (0,) * a.ndim, pipeline_mode=pl.Buffered(1))
    return pl.pallas_call(
        functools.partial(_inproj_kernel, tiles_per_seq=seq_len // tm),
        grid=(n // tm,),
        in_specs=[
            pl.BlockSpec((tm, D_MODEL), lambda i: (i, 0)),
            pl.BlockSpec((1, D_MODEL), lambda i: (0, 0)),
            pl.BlockSpec((D_MODEL, D_PACKED), lambda i: (0, 0), pipeline_mode=pl.Buffered(1)),
            pl.BlockSpec((CONV_K, GDN_QKV), lambda i: (0, 0)),
        ] + [const(a) for a in s5_ins],
        out_specs=tuple(pl.BlockSpec((tm, w), lambda i: (i, 0)) for w in widths),
        out_shape=out_shape,
        scratch_shapes=[pltpu.VMEM((CONV_PAD, GDN_QKV), F32),
                        pltpu.VMEM((1, S5_NSTATE), F32),
                        pltpu.VMEM((tm, BRANCH_W), F32)],
        compiler_params=pltpu.CompilerParams(dimension_semantics=("arbitrary",),
                                             vmem_limit_bytes=VMEM_LIMIT_BYTES),
        name="inproj",
    )(x2d, g, w_packed, conv_w, *s5_ins)


W_IN_SIZES = (N_HEADS * GLA_DK, N_HEADS * GLA_DK, BRANCH_W, GLA_RANK, BRANCH_W,
              GDN_QKV, N_HEADS, N_HEADS, BRANCH_W,
              N_HEADS * RET_DK, N_HEADS * RET_DK, BRANCH_W, BRANCH_W,
              BRANCH_W, BRANCH_W, N_BRANCH * D_MODEL)
D_IN = sum(W_IN_SIZES)
W_IN_ORDER = (0, 1, 2, 4, 5, 8, 9, 10, 11, 12, 13, 14, 15, 3, 6, 7)
PACK_ROWS = 128


def _pack_kernel(w_ref, o_ref):
    src = np.concatenate([[0], np.cumsum(W_IN_SIZES)])
    dst = 0
    for idx in W_IN_ORDER:
        width = W_IN_SIZES[idx]
        o_ref[:, dst:dst + width] = w_ref[:, int(src[idx]):int(src[idx]) + width].astype(o_ref.dtype)
        dst += width
    o_ref[:, dst:] = jnp.zeros((o_ref.shape[0], o_ref.shape[1] - dst), o_ref.dtype)


def _pack_w_in(w_in):
    depth = w_in.shape[0]
    return pl.pallas_call(
        _pack_kernel,
        grid=(depth, D_MODEL // PACK_ROWS),
        in_specs=[pl.BlockSpec((None, PACK_ROWS, D_IN), lambda l, i: (l, i, 0))],
        out_specs=pl.BlockSpec((None, PACK_ROWS, D_PACKED), lambda l, i: (l, i, 0)),
        out_shape=jax.ShapeDtypeStruct((depth, D_MODEL, D_PACKED), BF16),
        compiler_params=pltpu.CompilerParams(dimension_semantics=("arbitrary", "arbitrary"),
                                             vmem_limit_bytes=VMEM_LIMIT_BYTES),
        name="pack_w_in",
    )(w_in)


def _linattn_chunk(q_e, k_e, k_s, v, dec_row, st_ref, dk):
    hk = N_HEADS * dk
    hv = v.shape[2]
    dv = hv // N_HEADS
    r64 = lax.broadcasted_iota(jnp.int32, (1, CHUNK, N_HEADS * CHUNK), 1)
    c64 = lax.broadcasted_iota(jnp.int32, (1, CHUNK, N_HEADS * CHUNK), 2) % CHUNK
    scores = jnp.where(r64 >= c64, _bmm_nt(q_e, _block_diag_rows(k_e.astype(BF16), dk)), 0.0)
    st = st_ref[...]
    o = _bmm(scores, _block_diag_rows(v.astype(BF16), dv)) + _bmm_nt(q_e, st)
    kv_t = _bmm(jnp.swapaxes(v.astype(F32), 1, 2), k_s)
    smask = (lax.broadcasted_iota(jnp.int32, (1, hv, hk), 1) // dv
             == lax.broadcasted_iota(jnp.int32, (1, hv, hk), 2) // dk)
    st_ref[...] = st * dec_row + jnp.where(smask, kv_t, 0.0)
    return o


def _store_heads(y_ref, rows, o, z, gn_ref, kind, dv):
    for h in range(N_HEADS):
        hs = slice(h * dv, (h + 1) * dv)
        y = _head_norm(o[:, :, hs], kind) * gn_ref[...] * _silu(z[:, :, hs])
        y_ref[:, rows, hs] = y.astype(y_ref.dtype)


def _gla_chunk(rows, p_ref, s_ref, wa2_ref, ba_ref, gn_ref, y_ref, st_ref):
    nb = p_ref.shape[0]
    hk = N_HEADS * GLA_DK
    tri_b = jnp.broadcast_to(_tri_incl(CHUNK)[None], (nb, CHUNK, CHUNK))
    q = p_ref[:, rows, 0:hk].astype(F32) * (GLA_DK ** -0.5)
    k = p_ref[:, rows, hk:2 * hk].astype(F32)
    v = p_ref[:, rows, 2 * hk:2 * hk + BRANCH_W]
    z = p_ref[:, rows, 2 * hk + BRANCH_W:2 * hk + 2 * BRANCH_W].astype(F32)
    logits = jnp.stack([_bdot(s_ref[b, rows, 0:GLA_RANK], wa2_ref[...]) for b in range(nb)]) + ba_ref[...]
    log_a = _log_sigmoid(logits) * (1.0 / GLA_TAU)
    cum = _cumsum_rows(tri_b, log_a)
    last = cum[:, CHUNK - 1:CHUNK, :]
    o = _linattn_chunk(q * jnp.exp(cum), k * jnp.exp(-cum), k * jnp.exp(last - cum), v,
                       jnp.exp(last), st_ref, GLA_DK)
    _store_heads(y_ref, rows, o, z, gn_ref, "rms", GLA_DV)


def _ret_chunk(rows, p_ref, cos_ref, sin_ref, qe_ref, ke_ref, ks_ref, dec_ref, gn_ref, y_ref, st_ref):
    nb = p_ref.shape[0]
    hk = N_HEADS * RET_DK
    half = RET_DK // 2
    first_half = (lax.broadcasted_iota(jnp.int32, (1, CHUNK, hk), 2) % RET_DK) < half

    def rotary(x, cos, sin):
        swapped = jnp.where(first_half, pltpu.roll(x, hk - half, 2), pltpu.roll(x, half, 2))
        return x * cos + swapped * sin

    cos = cos_ref[rows, :][None]
    sin = sin_ref[rows, :][None]
    q = rotary(p_ref[:, rows, 0:hk].astype(F32), cos, sin)
    k = rotary(p_ref[:, rows, hk:2 * hk].astype(F32), cos, sin) * (RET_DK ** -0.5)
    v = p_ref[:, rows, 2 * hk:2 * hk + BRANCH_W]
    z = p_ref[:, rows, 2 * hk + BRANCH_W:2 * hk + 2 * BRANCH_W].astype(F32)
    o = _linattn_chunk(q * qe_ref[...][None], k * ke_ref[...][None], k * ks_ref[...][None], v,
                       jnp.broadcast_to(dec_ref[...][None], (nb, 1, hk)), st_ref, RET_DK)
    _store_heads(y_ref, rows, o, z, gn_ref, "group", RET_DV)


N_GLA_IN = 5
N_RET_IN = 8


def _gla_ret_kernel(*refs):
    gla_in, ret_in = refs[:N_GLA_IN], refs[N_GLA_IN:N_GLA_IN + N_RET_IN]
    ya_ref, yc_ref, sta_ref, stc_ref = refs[N_GLA_IN + N_RET_IN:]

    @pl.when(pl.program_id(0) == 0)
    def _():
        sta_ref[...] = jnp.zeros_like(sta_ref)
        stc_ref[...] = jnp.zeros_like(stc_ref)

    for c in range(ya_ref.shape[1] // CHUNK):
        rows = slice(c * CHUNK, (c + 1) * CHUNK)
        _gla_chunk(rows, *gla_in, ya_ref, sta_ref)
        _ret_chunk(rows, *ret_in, yc_ref, stc_ref)


def _gla_ret(pa, ps, pc, w_a2, b_a, gla_norm, ret_tables, ret_norm):
    bsz, t, _ = pa.shape
    tt = min(ATTN_TIME_TILE, t)
    hka, hkc = N_HEADS * GLA_DK, N_HEADS * RET_DK
    cos_t, sin_t, qe, ke, ks, dec = ret_tables
    ins = (pa, ps, w_a2.astype(BF16), b_a.reshape(1, hka), gla_norm.reshape(1, GLA_DV),
           pc, cos_t, sin_t, qe, ke, ks, dec, ret_norm.reshape(1, RET_DV))
    pos_spec = pl.BlockSpec((tt, hkc), lambda i: (i, 0))
    in_specs = [_tok_all(bsz, tt, SEG_A), _tok_all(bsz, tt, SMALL_W), _const1((GLA_RANK, hka)),
                _const1((1, hka)), _const1((1, GLA_DV)),
                _tok_all(bsz, tt, SEG_C), pos_spec, pos_spec, _const1((CHUNK, hkc)), _const1((CHUNK, hkc)),
                _const1((CHUNK, hkc)), _const1((1, hkc)), _const1((1, RET_DV))]
    y_spec = pl.BlockSpec((bsz, tt, BRANCH_W), lambda i: (0, i, 0))
    y_shape = jax.ShapeDtypeStruct((bsz, t, BRANCH_W), BF16)
    return pl.pallas_call(
        _gla_ret_kernel,
        grid=(t // tt,),
        in_specs=in_specs,
        out_specs=(y_spec, y_spec),
        out_shape=(y_shape, y_shape),
        scratch_shapes=[pltpu.VMEM((bsz, BRANCH_W, hka), F32), pltpu.VMEM((bsz, BRANCH_W, hkc), F32)],
        compiler_params=pltpu.CompilerParams(dimension_semantics=("arbitrary",),
                                             vmem_limit_bytes=VMEM_LIMIT_BYTES),
        name="gla_ret",
    )(*ins)


def _ret_tables(t):
    half = RET_DK // 2
    inv_freq = ROPE_BASE ** (-jnp.arange(half, dtype=F32) / half)
    ang = jnp.arange(t, dtype=F32)[:, None] * inv_freq
    cos, sin = jnp.cos(ang), jnp.sin(ang)
    cos_t = jnp.tile(jnp.concatenate([cos, cos], axis=1), (1, N_HEADS))
    sin_t = jnp.tile(jnp.concatenate([-sin, sin], axis=1), (1, N_HEADS))
    log_gamma = jnp.log1p(-jnp.exp2(-5.0 - jnp.arange(N_HEADS, dtype=F32)))
    lg = jnp.repeat(log_gamma, RET_DK)[None, :]
    cum = jnp.arange(1, CHUNK + 1, dtype=F32)[:, None] * lg
    last = cum[-1:, :]
    return cos_t, sin_t, jnp.exp(cum), jnp.exp(-cum), jnp.exp(last - cum), jnp.exp(last)


GDN_HD = N_HEADS * GDN_DK
GDN_PAIR = 2 * GDN_DK


def _head_lanes(x, lane0, width):
    n, rows = x.shape[0], x.shape[1]
    total = N_HEADS * width
    lane = lax.broadcasted_iota(jnp.int32, (1, 1, total), 2)
    out = jnp.broadcast_to(x[:, :, lane0:lane0 + 1], (n, rows, total))
    for h in range(1, N_HEADS):
        out = jnp.where(lane >= h * width, jnp.broadcast_to(x[:, :, lane0 + h:lane0 + h + 1], (n, rows, total)), out)
    return out


def _gdn_kernel(p_ref, s_ref, gpar_ref, gn_ref, y_ref, st_ref):
    nb, tt = p_ref.shape[0], p_ref.shape[1]
    nc = tt // CHUNK
    n = nc * nb

    @pl.when(pl.program_id(0) == 0)
    def _():
        st_ref[...] = jnp.zeros_like(st_ref)

    def chunks(c_lo, c_hi, ref=p_ref):
        return jnp.concatenate([ref[:, c * CHUNK:(c + 1) * CHUNK, c_lo:c_hi] for c in range(nc)],
                               axis=0).astype(F32)

    q = chunks(0, GDN_HD)
    k = chunks(GDN_HD, 2 * GDN_HD)
    v = chunks(2 * GDN_HD, 3 * GDN_HD)
    z = chunks(GDN_QKV, GDN_QKV + BRANCH_W)
    s = chunks(0, SMALL_W, s_ref)

    g = -jnp.exp(gpar_ref[0:1, :]) * _softplus(s + gpar_ref[1:2, :])
    beta = _head_lanes(_sigmoid(s), GDN_B_OFF, GDN_DK)

    r64 = lax.broadcasted_iota(jnp.int32, (1, CHUNK, N_HEADS * CHUNK), 1)
    c64 = lax.broadcasted_iota(jnp.int32, (1, CHUNK, N_HEADS * CHUNK), 2) % CHUNK
    tri_b = jnp.broadcast_to(_tri_incl(CHUNK)[None], (n, CHUNK, CHUNK))
    dlog = _cumsum_rows(tri_b, jnp.where(r64 > c64, _head_lanes(g, GDN_A_OFF, CHUNK), 0.0))
    decay = jnp.where(r64 >= c64, jnp.exp(dlog), 0.0)
    gcol = _head_lanes(_cumsum_rows(tri_b, g), GDN_A_OFF, GDN_DK)
    glast = gcol[:, CHUNK - 1:CHUNK, :]
    egc = jnp.exp(gcol)
    dec = jnp.exp(glast)

    kb = k * beta
    q_e = q * egc
    k_st = jnp.swapaxes(k * jnp.exp(glast - gcol), 1, 2).astype(BF16)
    kq = _bmm_nt(jnp.concatenate([kb, q], axis=1), _block_diag_rows(k.astype(BF16), GDN_DK))
    low = jnp.where(r64 > c64, kq[:, 0:CHUNK, :] * decay, 0.0)
    attn = kq[:, CHUNK:, :] * decay

    m = -low
    inv = jnp.where(r64 == c64, 1.0, 0.0) + m
    levels = int(math.log2(CHUNK)) - 1
    for lvl in range(levels):
        mbd = _block_diag_rows(m.astype(BF16), CHUNK)
        if lvl == 0:
            m = _bmm(m, mbd)
        else:
            both = _bmm(jnp.concatenate([m, inv], axis=1), mbd)
            m = both[:, 0:CHUNK, :]
            inv = inv + both[:, CHUNK:, :]
    inv = inv + _bmm(inv, _block_diag_rows(m.astype(BF16), CHUNK))

    vb = v * beta
    kbe = kb * egc
    us, ws = [], []
    for h in range(N_HEADS):
        hs = slice(h * GDN_DV, (h + 1) * GDN_DV)
        uw = _bmm(inv[:, :, h * CHUNK:(h + 1) * CHUNK], jnp.concatenate([vb[:, :, hs], kbe[:, :, hs]], axis=-1))
        us.append(uw[:, :, 0:GDN_DV])
        ws.append(uw[:, :, GDN_DV:])
    u = jnp.concatenate(us, axis=-1)
    wq = jnp.concatenate([jnp.concatenate(ws, axis=-1), q_e], axis=1)

    pair_mask = (lax.broadcasted_iota(jnp.int32, (1, GDN_PAIR, GDN_PAIR), 1) // GDN_DK
                 == lax.broadcasted_iota(jnp.int32, (1, GDN_PAIR, GDN_PAIR), 2) // GDN_DV)
    for c in range(nc):
        bs = slice(c * nb, (c + 1) * nb)
        ws_parts = []
        for pr in range(N_HEADS // 2):
            ps_ = slice(pr * GDN_PAIR, (pr + 1) * GDN_PAIR)
            ws_parts.append(_bmm(wq[bs, :, ps_], st_ref[:, pr]))
        w_s = jnp.concatenate(ws_parts, axis=-1)
        v_new = u[bs] - w_s[:, 0:CHUNK, :]
        o = w_s[:, CHUNK:, :] + _bmm(attn[bs], _block_diag_rows(v_new.astype(BF16), GDN_DV))
        for pr in range(N_HEADS // 2):
            ps_ = slice(pr * GDN_PAIR, (pr + 1) * GDN_PAIR)
            kv = _bmm(k_st[bs, ps_, :], v_new[:, :, ps_])
            st_ref[:, pr] = st_ref[:, pr] * dec[bs, :, ps_] + jnp.where(pair_mask, kv, 0.0)
        _store_heads(y_ref, slice(c * CHUNK, (c + 1) * CHUNK), o, z[bs], gn_ref, "rms", GDN_DV)


def _gdn(pb, ps, a_log, dt_bias, g_norm):
    bsz, t, _ = pb.shape
    tt = min(GDN_TIME_TILE, t)
    lane_pad = lambda x: jnp.pad(x.reshape(1, N_HEADS), ((0, 0), (GDN_A_OFF, SMALL_W - GDN_A_OFF - N_HEADS)))
    gpar = jnp.concatenate([lane_pad(a_log), lane_pad(dt_bias)], axis=0)
    ins = (pb, ps, gpar, g_norm.reshape(1, GDN_DV))
    in_specs = [_tok_all(bsz, tt, SEG_B), _tok_all(bsz, tt, SMALL_W), _const1((2, SMALL_W)), _const1((1, GDN_DV))]
    scratch = [pltpu.VMEM((bsz, N_HEADS // 2, GDN_PAIR, GDN_PAIR), F32)]
    return _all_batch_call(_gdn_kernel, ins, in_specs, scratch, bsz, t, tt, "gdn")


def _expand_lanes(x, reps):
    w = x.shape[-1]
    sel = jnp.tile(jnp.eye(w, dtype=F32), (1, reps))
    return jnp.dot(x, sel, precision=lax.Precision.HIGHEST)


def _s5_tables(lam_re, lam_im, log_dt, b_re, b_im, c_re, c_im, tt):
    g, p, gi = S5_GROUPS, S5_STATE, S5_GROUP
    hg = g // 2
    dt = jnp.exp(log_dt)[:, None]

    def a_pow(n):
        n = jnp.asarray(n, F32)[..., None, None]
        mag = jnp.exp(lam_re * dt * n)
        return mag * jnp.cos(lam_im * dt * n), mag * jnp.sin(lam_im * dt * n)

    abar_re, abar_im = a_pow(1.0)
    nr, ni = abar_re - 1.0, abar_im
    den = jnp.square(lam_re) + jnp.square(lam_im)
    coef_re = (nr * lam_re + ni * lam_im) / den
    coef_im = (ni * lam_re - nr * lam_im) / den
    bbar_re = coef_re[..., None] * b_re - coef_im[..., None] * b_im
    bbar_im = coef_re[..., None] * b_im + coef_im[..., None] * b_re

    def lanes(re, im):
        lead = re.shape[:-2]
        parts = [re[..., :hg, :].reshape(lead + (hg * p,)), im[..., :hg, :].reshape(lead + (hg * p,)),
                 re[..., hg:, :].reshape(lead + (hg * p,)), im[..., hg:, :].reshape(lead + (hg * p,))]
        return jnp.concatenate(parts, axis=-1)

    grp_row = jnp.arange(BRANCH_W)[:, None] // gi % hg
    b_mask = grp_row == (jnp.arange(hg * p)[None, :] // p)
    bt = lambda b: _expand_lanes(jnp.swapaxes(b, 1, 2).reshape(BRANCH_W, p), hg) * b_mask
    b_mat = jnp.concatenate([bt(bbar_re), bt(bbar_im)], axis=1).reshape(2, S5_HALF_W, S5_HALF)
    st_row = jnp.arange(hg * p)[:, None] // p
    c_mask = st_row == (jnp.arange(S5_HALF_W)[None, :] // gi)
    ct = lambda c: (_expand_lanes(jnp.swapaxes(c, 1, 2).reshape(2 * hg * p, gi), hg).reshape(2, hg * p, S5_HALF_W)
                    * c_mask)
    c_mat = jnp.concatenate([ct(c_re), ct(-c_im)], axis=1)
    pr, pi = a_pow(jnp.arange(S5_SUB, dtype=F32))
    cb = (jnp.einsum("gip,tgp,gpj->tgji", c_re, pr, bbar_re) - jnp.einsum("gip,tgp,gpj->tgji", c_re, pi, bbar_im)
          - jnp.einsum("gip,tgp,gpj->tgji", c_im, pr, bbar_im) - jnp.einsum("gip,tgp,gpj->tgji", c_im, pi, bbar_re))
    t_mask = (jnp.arange(S5_HALF_W)[:, None] // gi) == (jnp.arange(S5_HALF_W)[None, :] // gi)
    taps = (_expand_lanes(cb.reshape(S5_SUB * BRANCH_W, gi), hg).reshape(S5_SUB, 2, S5_HALF_W, S5_HALF_W)
            * t_mask)
    rho = jnp.arange(S5_SUB, dtype=F32)
    w_in = lanes(*a_pow(S5_SUB - 1 - rho))
    w_out = lanes(*a_pow(rho + 1))
    nblk = tt // S5_SUB
    steps = int(math.log2(nblk))
    scan_pw = lanes(*a_pow(S5_SUB * (2.0 ** jnp.arange(steps, dtype=F32))))
    carry_pw = lanes(*a_pow(S5_SUB * jnp.arange(1, nblk + 1, dtype=F32)))
    return (b_mat.astype(BF16), c_mat.astype(BF16), taps.astype(BF16), w_in, w_out, scan_pw, carry_pw)


def _cmul(a_re, a_im, b_re, b_im):
    return a_re * b_re - a_im * b_im, a_re * b_im + a_im * b_re


def _halves(x):
    q = S5_HALF // 2
    return [(x[:, 0:q], x[:, q:2 * q]), (x[:, 2 * q:3 * q], x[:, 3 * q:4 * q])]


def _cscale(x, w):
    out = []
    for (xr, xi), (wr, wi) in zip(_halves(x), _halves(w)):
        out.extend(_cmul(xr, xi, wr, wi))
    return jnp.concatenate(out, axis=1)


N_S5_TABLES = 10


def _s5_tile(u, z, tabs, carry_ref, acc_ref, between=lambda: None):
    bm_ref, cm_ref, taps_ref, win_ref, wout_ref, spw_ref, cpw_ref, d_ref, wglu_ref, bglu_ref = tabs
    tt = u.shape[0]
    nblk = tt // S5_SUB
    hw = S5_HALF_W
    dot = functools.partial(jnp.dot, preferred_element_type=F32)

    r_i = _iota2((tt, tt), 0)
    t_i = _iota2((tt, tt), 1)
    perm = (t_i == S5_SUB * (r_i % nblk) + r_i // nblk).astype(BF16)
    unperm = (r_i == S5_SUB * (t_i % nblk) + t_i // nblk).astype(BF16)
    up = dot(perm, u).astype(BF16)
    between()

    bu = jnp.concatenate([dot(up[:, 0:hw], bm_ref[0]), dot(up[:, hw:], bm_ref[1])], axis=1)
    for half in range(2):
        hs = slice(half * hw, (half + 1) * hw)
        acc_ref[:, hs] = dot(up[:, hs], taps_ref[0, half])
        for tau in range(1, S5_SUB):
            acc_ref[tau * nblk:, hs] += dot(up[0:tt - tau * nblk, hs], taps_ref[tau, half])
    between()
    s = bu[(S5_SUB - 1) * nblk:, :]
    for rho in range(S5_SUB - 1):
        s = s + _cscale(bu[rho * nblk:(rho + 1) * nblk], win_ref[rho:rho + 1, :])

    row = _iota2((nblk, S5_NSTATE), 0)
    x = s
    for kk in range(int(math.log2(nblk))):
        sh = 1 << kk
        shifted = jnp.where(row >= sh, pltpu.roll(x, sh, 0), 0.0)
        x = x + _cscale(shifted, spw_ref[kk:kk + 1, :])
    carry = jnp.broadcast_to(carry_ref[...], (nblk, S5_NSTATE))
    x = x + _cscale(carry, cpw_ref[...])
    carry_ref[...] = x[nblk - 1:nblk, :]
    x_prev = jnp.where(row >= 1, pltpu.roll(x, 1, 0), carry)
    between()

    zc = jnp.concatenate([_cscale(x_prev, wout_ref[rho:rho + 1, :]).astype(BF16) for rho in range(S5_SUB)], axis=0)
    acc_ref[:, 0:hw] += dot(zc[:, 0:S5_HALF], cm_ref[0])
    acc_ref[:, hw:] += dot(zc[:, S5_HALF:], cm_ref[1])
    between()

    hi, lo = _split2(acc_ref[...])
    y = dot(unperm, hi) + dot(unperm, lo) + d_ref[...] * u.astype(F32)
    between()
    y = 0.5 * y * (1.0 + jnp.tanh(math.sqrt(2.0 / math.pi) * (y + 0.044715 * (y * y * y))))
    y = y * _sigmoid(_bdot(y, wglu_ref[...]) + bglu_ref[...])
    return y * _silu(z)


def _s5_kernel(p_ref, *refs):
    tabs, (y_ref, carry_ref, acc_ref) = refs[:N_S5_TABLES], refs[N_S5_TABLES:]

    @pl.when(pl.program_id(1) == 0)
    def _():
        carry_ref[...] = jnp.zeros_like(carry_ref)

    y = _s5_tile(p_ref[:, 0:BRANCH_W], p_ref[:, BRANCH_W:2 * BRANCH_W].astype(F32), tabs, carry_ref, acc_ref)
    y_ref[...] = y.astype(y_ref.dtype)


def _s5(pd, tables, d, w_glu, b_glu):
    bsz, t, _ = pd.shape
    tt = min(S5_TIME_TILE, t)
    nblk = tt // S5_SUB
    steps = int(math.log2(nblk))
    b_mat, c_mat, taps, w_in, w_out, scan_pw, carry_pw = tables
    ins = (pd, b_mat, c_mat, taps, w_in, w_out, scan_pw, carry_pw, d.reshape(1, BRANCH_W),
           w_glu.astype(BF16), b_glu.reshape(1, BRANCH_W))
    const = lambda shape: pl.BlockSpec(shape, lambda b, i: (0,) * len(shape))
    in_specs = [pl.BlockSpec((None, tt, SEG_D), lambda b, i: (b, i, 0)),
                const((2, S5_HALF_W, S5_HALF)), const((2, S5_HALF, S5_HALF_W)),
                const((S5_SUB, 2, S5_HALF_W, S5_HALF_W)), const((S5_SUB, S5_NSTATE)),
                const((S5_SUB, S5_NSTATE)), const((steps, S5_NSTATE)), const((nblk, S5_NSTATE)),
                const((1, BRANCH_W)), const((BRANCH_W, BRANCH_W)), const((1, BRANCH_W))]
    return pl.pallas_call(
        _s5_kernel,
        grid=(bsz, t // tt),
        in_specs=in_specs,
        out_specs=pl.BlockSpec((None, tt, BRANCH_W), lambda b, i: (b, i, 0)),
        out_shape=jax.ShapeDtypeStruct((bsz, t, BRANCH_W), BF16),
        scratch_shapes=[pltpu.VMEM((1, S5_NSTATE), F32), pltpu.VMEM((tt, BRANCH_W), F32)],
        compiler_params=pltpu.CompilerParams(dimension_semantics=("arbitrary", "arbitrary"),
                                             vmem_limit_bytes=VMEM_LIMIT_BYTES),
        name="s5",
    )(*ins)


def _merge_kernel(x_ref, ya_ref, yb_ref, yc_ref, yd_ref, g_ref, wb_ref, wo_ref, fn_ref, o_ref, *,
                  final_norm):
    merged = jnp.zeros((x_ref.shape[0], D_MODEL), F32)
    for n, y_ref in enumerate((ya_ref, yb_ref, yc_ref, yd_ref)):
        proj = jnp.dot(y_ref[...], wb_ref[n], preferred_element_type=F32)
        merged = merged + _sigmoid(g_ref[:, n * D_MODEL:(n + 1) * D_MODEL].astype(F32)) * proj
    out = x_ref[...] + _bdot(merged, wo_ref[...])
    if final_norm:
        out = out * lax.rsqrt(jnp.mean(out * out, axis=-1, keepdims=True) + EPS) * fn_ref[...]
    o_ref[...] = out


def _merge(x2d, ys, gates, w_branch, w_out, fnorm, final_norm):
    n = x2d.shape[0]
    tm = min(ROW_TILE, n)
    row = lambda w: pl.BlockSpec((tm, w), lambda i: (i, 0))
    return pl.pallas_call(
        functools.partial(_merge_kernel, final_norm=final_norm),
        grid=(n // tm,),
        in_specs=[row(D_MODEL), row(BRANCH_W), row(BRANCH_W), row(BRANCH_W), row(BRANCH_W),
                  row(SEG_G),
                  pl.BlockSpec((N_BRANCH, BRANCH_W, D_MODEL), lambda i: (0, 0, 0)),
                  pl.BlockSpec((D_MODEL, D_MODEL), lambda i: (0, 0)),
                  pl.BlockSpec((1, D_MODEL), lambda i: (0, 0))],
        out_specs=row(D_MODEL),
        out_shape=jax.ShapeDtypeStruct((n, D_MODEL), F32),
        compiler_params=pltpu.CompilerParams(dimension_semantics=("arbitrary",),
                                             vmem_limit_bytes=VMEM_LIMIT_BYTES),
        name="merge",
    )(x2d, *ys, gates, w_branch, w_out, fnorm.reshape(1, D_MODEL))


def kernel(x, norm_g, w_in, w_branch, w_out, gla_w_a2, gla_b_a, gla_norm, gdn_conv, gdn_a_log,
           gdn_dt_bias, gdn_norm, ret_norm, s5_lambda_re, s5_lambda_im, s5_log_dt, s5_b_re, s5_b_im,
           s5_c_re, s5_c_im, s5_d, s5_w_glu, s5_b_glu, final_norm):
    bsz, t, _ = x.shape
    depth = w_in.shape[0]
    n = bsz * t
    ret_tables = _ret_tables(t)
    w_packed, w_branch_b, w_out_b = _pack_w_in(w_in), w_branch.astype(BF16), w_out.astype(BF16)
    x2d = x.reshape(n, D_MODEL)
    for l in range(depth):
        s5_ins = _s5_tables(s5_lambda_re[l], s5_lambda_im[l], s5_log_dt[l], s5_b_re[l], s5_b_im[l],
                            s5_c_re[l], s5_c_im[l], min(S5_TIME_TILE, t)) + (
            s5_d[l].reshape(1, BRANCH_W), s5_w_glu[l].astype(BF16), s5_b_glu[l].reshape(1, BRANCH_W))
        pa, pb, pc, yd, pg, ps = _inproj(x2d, norm_g[l].reshape(1, D_MODEL), w_packed[l], gdn_conv[l], s5_ins, t)
        tok = lambda a: a.reshape(bsz, t, a.shape[-1])
        ps3 = tok(ps)
        ya, yc = _gla_ret(tok(pa), ps3, tok(pc), gla_w_a2[l], gla_b_a[l], gla_norm[l], ret_tables, ret_norm[l])
        yb = _gdn(tok(pb), ps3, gdn_a_log[l], gdn_dt_bias[l], gdn_norm[l])
        ys = tuple(y.reshape(n, BRANCH_W) for y in (ya, yb, yc)) + (yd,)
        x2d = _merge(x2d, ys, pg, w_branch_b[l], w_out_b[l], final_norm, final_norm=(l == depth - 1))
    return x2d.reshape(bsz, t, D_MODEL)
```
